```python
import jax, jax.numpy as jnp
from jax import lax
import numpy as np

D_MODEL = 1024
BATCH = 4
SEQ = 4096
DEPTH = 2

GRID_W = 64
CTX_LEN = 256
D_FF = 2816
CONV_WIDTH = 31
CONV_PAD = (CONV_WIDTH - 1) // 2
N_Q_HEADS = 16
N_KV_HEADS = 4
HEAD_DIM = 64
WINDOW = 128
BLOCK = 128
ROPE_BASE = 10000.0
NORM_EPS = 1e-6
N_MOD = 9
N_CONV_LAYERS = (DEPTH + 1) // 2
N_ATTN_LAYERS = DEPTH // 2
Q_DIM = N_Q_HEADS * HEAD_DIM
KV_DIM = N_KV_HEADS * HEAD_DIM

kernel_name = "hybrid_conv_swa_macaron_prefix_trunk"


def rmsnorm(x, g):
    xf = x.astype(jnp.float32)
    y = xf * lax.rsqrt(jnp.mean(xf * xf, axis=-1, keepdims=True) + NORM_EPS)
    return y.astype(x.dtype) * g


def modulate(x, g, shift, scale):
    return rmsnorm(x, g) * (1 + scale) + shift


def swiglu(h, w_in, w_out):
    gu = h @ w_in
    gate, up = jnp.split(gu, 2, axis=-1)
    return (jax.nn.silu(gate) * up) @ w_out


def conformer_conv(h, pw1_w, pw1_b, dw_w, dw_b, ln_g, ln_b, pw2_w, pw2_b):
    a = h @ pw1_w + pw1_b
    u, g = jnp.split(a, 2, axis=-1)
    u = u * jax.nn.sigmoid(g)
    u = lax.conv_general_dilated(
        u, dw_w[:, None, :].astype(u.dtype), window_strides=(1,),
        padding=[(CONV_PAD, CONV_PAD)], dimension_numbers=("NWC", "WIO", "NWC"),
        feature_group_count=D_MODEL) + dw_b
    uf = u.astype(jnp.float32)
    mu = jnp.mean(uf, axis=-1, keepdims=True)
    var = jnp.mean(jnp.square(uf - mu), axis=-1, keepdims=True)
    u = ((uf - mu) * lax.rsqrt(var + NORM_EPS)).astype(u.dtype) * ln_g + ln_b
    return jax.nn.silu(u) @ pw2_w + pw2_b


def axial_rope_tables(n_tok):
    n_rows = n_tok // GRID_W
    row = jnp.repeat(jnp.arange(n_rows), GRID_W).astype(jnp.float32)
    col = jnp.tile(jnp.arange(GRID_W), n_rows).astype(jnp.float32)
    n_freq = HEAD_DIM // 4
    inv = ROPE_BASE ** (-jnp.arange(n_freq, dtype=jnp.float32) / n_freq)
    ang = jnp.stack([row[:, None] * inv, col[:, None] * inv], axis=1)
    return jnp.cos(ang), jnp.sin(ang)


def apply_rope(x, cos, sin):
    n_freq = HEAD_DIM // 4
    xs = x.reshape(x.shape[:-1] + (2, 2, n_freq))
    x1 = xs[..., 0, :]
    x2 = xs[..., 1, :]
    bshape = (1, cos.shape[0]) + (1,) * (x.ndim - 3) + (2, n_freq)
    cs = cos.reshape(bshape).astype(x.dtype)
    sn = sin.reshape(bshape).astype(x.dtype)
    out = jnp.stack([x1 * cs - x2 * sn, x1 * sn + x2 * cs], axis=-2)
    return out.reshape(x.shape)


def windowed_gqa_sink(h_lat, h_ctx, w_qkv, w_o, sink, with_ctx_out):
    B, S, _ = h_lat.shape
    C = h_ctx.shape[1]
    G = N_Q_HEADS // N_KV_HEADS
    scale = HEAD_DIM ** -0.5

    def proj(h):
        qkv = h @ w_qkv
        q = qkv[..., :Q_DIM].reshape(h.shape[:2] + (N_KV_HEADS, G, HEAD_DIM))
        k = qkv[..., Q_DIM:Q_DIM + KV_DIM].reshape(h.shape[:2] + (N_KV_HEADS, HEAD_DIM))
        v = qkv[..., Q_DIM + KV_DIM:].reshape(h.shape[:2] + (N_KV_HEADS, HEAD_DIM))
        return q, k, v

    ql, kl, vl = proj(h_lat)
    qc, kc, vc = proj(h_ctx)
    cos, sin = axial_rope_tables(S)
    ql = apply_rope(ql, cos, sin) * scale
    kl = apply_rope(kl, cos, sin)
    pad = ((0, 0), (WINDOW, WINDOW), (0, 0), (0, 0))
    k_pad = jnp.pad(kl, pad)
    v_pad = jnp.pad(vl, pad)
    sink_col = sink.reshape(N_KV_HEADS, G, 1, 1).astype(jnp.float32)
    kw = BLOCK + 2 * WINDOW

    def one_block(n):
        start = n * BLOCK
        qb = lax.dynamic_slice_in_dim(ql, start, BLOCK, axis=1)
        kb = lax.dynamic_slice_in_dim(k_pad, start, kw, axis=1)
        vb = lax.dynamic_slice_in_dim(v_pad, start, kw, axis=1)
        i_pos = start + jnp.arange(BLOCK)
        j_pos = start - WINDOW + jnp.arange(kw)
        valid = ((jnp.abs(j_pos[None, :] - i_pos[:, None]) <= WINDOW)
                 & (j_pos[None, :] >= 0) & (j_pos[None, :] < S))
        s_win = jnp.einsum('bqhgd,bkhd->bhgqk', qb, kb).astype(jnp.float32)
        s_win = jnp.where(valid, s_win, -jnp.inf)
        s_ctx = jnp.einsum('bqhgd,bchd->bhgqc', qb, kc).astype(jnp.float32)
        sinks = jnp.broadcast_to(sink_col, s_win.shape[:-1] + (1,))
        p = jax.nn.softmax(jnp.concatenate([s_win, s_ctx, sinks], axis=-1), axis=-1).astype(vb.dtype)
        o = (jnp.einsum('bhgqk,bkhd->bqhgd', p[..., :kw], vb)
             + jnp.einsum('bhgqc,bchd->bqhgd', p[..., kw:kw + C], vc))
        return o.reshape(B, BLOCK, Q_DIM)

    o = lax.map(one_block, jnp.arange(S // BLOCK))
    y_lat = jnp.moveaxis(o, 0, 1).reshape(B, S, Q_DIM) @ w_o

    y_ctx = None
    if with_ctx_out:
        s = jnp.einsum('bqhgd,bkhd->bhgqk', qc * scale, kc).astype(jnp.float32)
        sinks = jnp.broadcast_to(sink_col, s.shape[:-1] + (1,))
        p = jax.nn.softmax(jnp.concatenate([s, sinks], axis=-1), axis=-1).astype(vc.dtype)
        oc = jnp.einsum('bhgqk,bkhd->bqhgd', p[..., :C], vc)
        y_ctx = oc.reshape(B, C, Q_DIM) @ w_o
    return y_lat, y_ctx


def setup_inputs(seed: int = 0) -> dict:
    key = jax.random.key(seed)
    ks = jax.random.split(key, 32)
    f32 = jnp.float32

    def nrm(k, shape, s):
        return jax.random.normal(k, shape, f32) * s

    D, F = D_MODEL, D_FF
    NC, NA = N_CONV_LAYERS, N_ATTN_LAYERS
    return {
        "x": nrm(ks[0], (BATCH, SEQ, D), 1.0),
        "c": nrm(ks[1], (BATCH, D), 1.0),
        "ctx": nrm(ks[2], (BATCH, CTX_LEN, D), 1.0),
        "c_ctx": nrm(ks[3], (D,), 1.0),
        "norm_g": 1.0 + nrm(ks[4], (DEPTH, 3, D), 0.05),
        "ada_w": nrm(ks[5], (DEPTH, D, N_MOD * D), D ** -0.5),
        "ada_b": nrm(ks[6], (DEPTH, N_MOD * D), 0.02),
        "ffn1_wi": nrm(ks[7], (DEPTH, D, 2 * F), D ** -0.5),
        "ffn1_wo": nrm(ks[8], (DEPTH, F, D), F ** -0.5),
        "ffn2_wi": nrm(ks[9], (DEPTH, D, 2 * F), D ** -0.5),
        "ffn2_wo": nrm(ks[10], (DEPTH, F, D), F ** -0.5),
        "conv_pw1_w": nrm(ks[11], (NC, D, 2 * D), D ** -0.5),
        "conv_pw1_b": nrm(ks[12], (NC, 2 * D), 0.02),
        "conv_dw_w": nrm(ks[13], (NC, CONV_WIDTH, D), CONV_WIDTH ** -0.5),
        "conv_dw_b": nrm(ks[14], (NC, D), 0.02),
        "conv_ln_g": 1.0 + nrm(ks[15], (NC, D), 0.05),
        "conv_ln_b": nrm(ks[16], (NC, D), 0.02),
        "conv_pw2_w": nrm(ks[17], (NC, D, D), D ** -0.5),
        "conv_pw2_b": nrm(ks[18], (NC, D), 0.02),
        "attn_w_qkv": nrm(ks[19], (NA, D, Q_DIM + 2 * KV_DIM), D ** -0.5),
        "attn_w_o": nrm(ks[20], (NA, Q_DIM, D), Q_DIM ** -0.5),
        "attn_sink": nrm(ks[21], (NA, N_Q_HEADS), 1.0),
        "final_g": 1.0 + nrm(ks[22], (D,), 0.05),
    }


def reference(x, c, ctx, c_ctx, norm_g, ada_w, ada_b, ffn1_wi, ffn1_wo, ffn2_wi, ffn2_wo,
              conv_pw1_w, conv_pw1_b, conv_dw_w, conv_dw_b, conv_ln_g, conv_ln_b,
              conv_pw2_w, conv_pw2_b, attn_w_qkv, attn_w_o, attn_sink, final_g):
    for i in range(DEPTH):
        last = i == DEPTH - 1
        m_lat = (jax.nn.silu(c) @ ada_w[i] + ada_b[i])[:, None, :]
        m_ctx = (jax.nn.silu(c_ctx) @ ada_w[i] + ada_b[i])[None, None, :]
        sh1, sc1, gt1, sh2, sc2, gt2, sh3, sc3, gt3 = jnp.split(m_lat, N_MOD, axis=-1)
        ch1, cs1, cg1, ch2, cs2, cg2, ch3, cs3, cg3 = jnp.split(m_ctx, N_MOD, axis=-1)
        g1, g2, g3 = norm_g[i, 0], norm_g[i, 1], norm_g[i, 2]

        x = x + 0.5 * gt1 * swiglu(modulate(x, g1, sh1, sc1), ffn1_wi[i], ffn1_wo[i])
        ctx = ctx + 0.5 * cg1 * swiglu(modulate(ctx, g1, ch1, cs1), ffn1_wi[i], ffn1_wo[i])

        hx = modulate(x, g2, sh2, sc2)
        hc = modulate(ctx, g2, ch2, cs2)
        j = i // 2
        if i % 2 == 0:
            conv_args = (conv_pw1_w[j], conv_pw1_b[j], conv_dw_w[j], conv_dw_b[j],
                         conv_ln_g[j], conv_ln_b[j], conv_pw2_w[j], conv_pw2_b[j])
            y_lat = conformer_conv(hx, *conv_args)
            y_ctx = None if last else conformer_conv(hc, *conv_args)
        else:
            y_lat, y_ctx = windowed_gqa_sink(hx, hc, attn_w_qkv[j], attn_w_o[j],
                                             attn_sink[j], not last)
        x = x + gt2 * y_lat

        x = x + 0.5 * gt3 * swiglu(modulate(x, g3, sh3, sc3), ffn2_wi[i], ffn2_wo[i])
        if not last:
            ctx = ctx + cg2 * y_ctx
            ctx = ctx + 0.5 * cg3 * swiglu(modulate(ctx, g3, ch3, cs3), ffn2_wi[i], ffn2_wo[i])
    return rmsnorm(x, final_g)
```

```python
import functools

import jax
import jax.numpy as jnp
from jax import lax
from jax.experimental import pallas as pl
from jax.experimental.pallas import tpu as pltpu

D_MODEL = 1024
D_FF = 2816
DEPTH = 2
GRID_W = 64
CONV_WIDTH = 31
CONV_PAD = (CONV_WIDTH - 1) // 2
N_Q_HEADS = 16
N_KV_HEADS = 4
GROUP = N_Q_HEADS // N_KV_HEADS
HEAD_DIM = 64
WINDOW = 128
ROPE_BASE = 10000.0
NORM_EPS = 1e-6
N_MOD = 9
Q_DIM = N_Q_HEADS * HEAD_DIM
KV_DIM = N_KV_HEADS * HEAD_DIM

LANES = 128
SUBLANES = 8
MOD_ROWS = 8
CTX_ROW = 4
FFN_CHUNK = 256
N_FFN_CHUNKS = D_FF // FFN_CHUNK
TOKEN_BLOCK = 512
HALO = 16
CONV_PHASES = 4
CONV_ROWS = CONV_PHASES * SUBLANES
N_SLABS = D_MODEL // LANES
Q_BLOCK = 128
MASK_VALUE = -1e30
VMEM_LIMIT_BYTES = 56 * 1024 * 1024

f32 = jnp.float32
bf16 = jnp.bfloat16


def _sigmoid(x):
    return 1.0 / (1.0 + jnp.exp(-x))


def _silu(x):
    return x * _sigmoid(x)


def _params(n_grid):
    return pltpu.CompilerParams(
        dimension_semantics=("arbitrary",) * n_grid,
        vmem_limit_bytes=VMEM_LIMIT_BYTES)


def _resident(shape):
    return pl.BlockSpec(shape, lambda *_: (0,) * len(shape), pipeline_mode=pl.Buffered(1))


def _modulated(x, mod_ref, g_ref, sub, row):
    shift = mod_ref[0, 3 * sub + 0, pl.ds(row, 1), :]
    scale = mod_ref[0, 3 * sub + 1, pl.ds(row, 1), :]
    gate = mod_ref[0, 3 * sub + 2, pl.ds(row, 1), :]
    g = g_ref[0, sub:sub + 1, :]
    y = x * lax.rsqrt(jnp.mean(x * x, axis=-1, keepdims=True) + NORM_EPS)
    h = (y * g) * (1.0 + scale) + shift
    return h.astype(bf16), gate


def _mod_row(is_ctx):
    return CTX_ROW if is_ctx else pl.program_id(0)


def _ada_kernel(c_ref, w_ref, b_ref, o_ref):
    s = _silu(c_ref[...]).astype(bf16)
    w = w_ref[0].astype(bf16)
    o_ref[0, 0] = jnp.dot(s, w, preferred_element_type=f32) + b_ref[0, 0]


def _ada_table(c, c_ctx, ada_w, ada_b):
    batch = c.shape[0]
    assert batch <= CTX_ROW < MOD_ROWS
    cvec = jnp.zeros((MOD_ROWS, D_MODEL), f32).at[:batch].set(c).at[CTX_ROW].set(c_ctx)
    bias = ada_b.reshape(DEPTH, N_MOD, 1, D_MODEL)
    return pl.pallas_call(
        _ada_kernel,
        grid=(DEPTH, N_MOD),
        in_specs=[
            pl.BlockSpec((MOD_ROWS, D_MODEL), lambda i, n: (0, 0)),
            pl.BlockSpec((1, D_MODEL, D_MODEL), lambda i, n: (i, 0, n)),
            pl.BlockSpec((1, 1, 1, D_MODEL), lambda i, n: (i, n, 0, 0)),
        ],
        out_specs=pl.BlockSpec((1, 1, MOD_ROWS, D_MODEL), lambda i, n: (i, n, 0, 0)),
        out_shape=jax.ShapeDtypeStruct((DEPTH, N_MOD, MOD_ROWS, D_MODEL), f32),
        compiler_params=_params(2),
        name="ada_table",
    )(cvec, ada_w, bias)


def _mod_spec(layer):
    return pl.BlockSpec((1, N_MOD, MOD_ROWS, D_MODEL), lambda *_: (layer, 0, 0, 0))


def _g_spec(layer):
    return pl.BlockSpec((1, 3, D_MODEL), lambda *_: (layer, 0, 0))


def _tok_spec(tm, width=D_MODEL):
    return pl.BlockSpec((1, tm, width), lambda b, t: (b, t, 0))


def _ffn_kernel(x_ref, mod_ref, g_ref, wi_ref, wo_ref, fg_ref, o_ref, *, sub, is_ctx, final):
    x = x_ref[0]
    h, gate = _modulated(x, mod_ref, g_ref, sub, _mod_row(is_ctx))
    acc = jnp.zeros(x.shape, f32)
    for c in range(N_FFN_CHUNKS):
        gu = jnp.dot(h, wi_ref[c], preferred_element_type=f32)
        a = _silu(gu[:, :FFN_CHUNK]) * gu[:, FFN_CHUNK:]
        acc = acc + jnp.dot(a.astype(bf16), wo_ref[c], preferred_element_type=f32)
    out = x + 0.5 * gate * acc
    if final:
        out = out * lax.rsqrt(jnp.mean(out * out, axis=-1, keepdims=True) + NORM_EPS) * fg_ref[...]
    o_ref[0] = out


def _ffn_weights(w_in, w_out):
    gate = w_in[:, :D_FF].reshape(D_MODEL, N_FFN_CHUNKS, FFN_CHUNK)
    up = w_in[:, D_FF:].reshape(D_MODEL, N_FFN_CHUNKS, FFN_CHUNK)
    wi = jnp.concatenate([gate, up], axis=2).transpose(1, 0, 2).astype(bf16)
    wo = w_out.reshape(N_FFN_CHUNKS, FFN_CHUNK, D_MODEL).astype(bf16)
    return wi, wo


def _ffn(x, mods, norm_g, wi, wo, final_g, *, layer, sub, is_ctx, final=False):
    batch, length, _ = x.shape
    tm = min(TOKEN_BLOCK, length)
    assert length % tm == 0
    return pl.pallas_call(
        functools.partial(_ffn_kernel, sub=sub, is_ctx=is_ctx, final=final),
        grid=(batch, length // tm),
        in_specs=[
            _tok_spec(tm), _mod_spec(layer), _g_spec(layer),
            _resident(wi.shape), _resident(wo.shape),
            pl.BlockSpec((1, D_MODEL), lambda b, t: (0, 0)),
        ],
        out_specs=_tok_spec(tm),
        out_shape=jax.ShapeDtypeStruct(x.shape, f32),
        compiler_params=_params(2),
        name=f"ffn_l{layer}_s{sub}_{'ctx' if is_ctx else 'lat'}",
    )(x, mods, norm_g, wi, wo, final_g.reshape(1, D_MODEL))


def _glu_kernel(x_ref, mod_ref, g_ref, w_ref, b_ref, u_ref, *, is_ctx):
    h, _ = _modulated(x_ref[0], mod_ref, g_ref, 1, _mod_row(is_ctx))
    a = jnp.dot(h, w_ref[...], preferred_element_type=f32) + b_ref[...]
    u_ref[0] = a[:, :D_MODEL] * _sigmoid(a[:, D_MODEL:])


def _conv_kernel(um_ref, up_ref, un_ref, x_ref, mod_ref, dww_ref, dwb_ref, lng_ref, lnb_ref,
                 w2_ref, b2_ref, o_ref, buf, conv, *, is_ctx):
    t = pl.program_id(1)
    tm = um_ref.shape[1]
    for j in range(N_SLABS):
        sl = slice(j * LANES, (j + 1) * LANES)
        buf[j, 0:HALO, :] = jnp.where(t > 0, up_ref[0, :, sl], 0.0)
        buf[j, HALO:HALO + tm, :] = um_ref[0, :, sl]
        buf[j, HALO + tm:, :] = jnp.where(t < pl.num_programs(1) - 1, un_ref[0, :, sl], 0.0)

    def step(i, carry):
        j = i % N_SLABS
        r0 = (i // N_SLABS) * CONV_ROWS
        taps = {}
        accs = []
        for p in range(CONV_PHASES):
            acc = jnp.zeros((SUBLANES, LANES), f32)
            for k in range(CONV_WIDTH):
                q = p + k - CONV_PAD
                if q not in taps:
                    taps[q] = buf[j, pl.ds(r0 + HALO + q, SUBLANES, stride=CONV_PHASES), :]
                acc = acc + taps[q] * dww_ref[j, k * SUBLANES:(k + 1) * SUBLANES, :]
            accs.append(acc)
        for p in range(CONV_PHASES):
            conv[j, pl.ds(r0 + p, SUBLANES, stride=CONV_PHASES), :] = accs[p]
        return carry

    lax.fori_loop(0, N_SLABS * (tm // CONV_ROWS), step, 0)

    u = jnp.concatenate([conv[j] for j in range(N_SLABS)], axis=1) + dwb_ref[...]
    mu = jnp.mean(u, axis=-1, keepdims=True)
    uc = u - mu
    var = jnp.mean(uc * uc, axis=-1, keepdims=True)
    un = (uc * lax.rsqrt(var + NORM_EPS)) * lng_ref[...] + lnb_ref[...]
    y = jnp.dot(_silu(un).astype(bf16), w2_ref[...], preferred_element_type=f32) + b2_ref[...]
    gate = mod_ref[0, 5, pl.ds(_mod_row(is_ctx), 1), :]
    o_ref[0] = x_ref[0] + gate * y


def _conv_mixer(x, mods, norm_g, pw1_w, pw1_b, dw_w, dw_b, ln_g, ln_b, pw2_w, pw2_b, *, layer, is_ctx):
    batch, length, _ = x.shape
    tm = min(TOKEN_BLOCK, length)
    assert length % tm == 0 and tm % HALO == 0 and tm % CONV_ROWS == 0
    tag = "ctx" if is_ctx else "lat"
    row = lambda v: v.reshape(1, -1)
    u = pl.pallas_call(
        functools.partial(_glu_kernel, is_ctx=is_ctx),
        grid=(batch, length // tm),
        in_specs=[
            _tok_spec(tm), _mod_spec(layer), _g_spec(layer),
            _resident(pw1_w.shape), pl.BlockSpec((1, 2 * D_MODEL), lambda b, t: (0, 0)),
        ],
        out_specs=_tok_spec(tm),
        out_shape=jax.ShapeDtypeStruct(x.shape, f32),
        compiler_params=_params(2),
        name=f"conv_glu_l{layer}_{tag}",
    )(x, mods, norm_g, pw1_w, row(pw1_b))

    halo_per_block = tm // HALO
    n_halo = length // HALO
    vec = pl.BlockSpec((1, D_MODEL), lambda b, t: (0, 0))
    dww = jnp.repeat(dw_w.reshape(CONV_WIDTH, N_SLABS, LANES).transpose(1, 0, 2), SUBLANES, axis=1)
    return pl.pallas_call(
        functools.partial(_conv_kernel, is_ctx=is_ctx),
        grid=(batch, length // tm),
        in_specs=[
            _tok_spec(tm),
            pl.BlockSpec((1, HALO, D_MODEL), lambda b, t: (b, jnp.maximum(t * halo_per_block - 1, 0), 0)),
            pl.BlockSpec((1, HALO, D_MODEL),
                         lambda b, t: (b, jnp.minimum((t + 1) * halo_per_block, n_halo - 1), 0)),
            _tok_spec(tm), _mod_spec(layer),
            pl.BlockSpec(dww.shape, lambda b, t: (0, 0, 0)),
            vec, vec, vec, _resident(pw2_w.shape), vec,
        ],
        out_specs=_tok_spec(tm),
        out_shape=jax.ShapeDtypeStruct(x.shape, f32),
        scratch_shapes=[pltpu.VMEM((N_SLABS, tm + 2 * HALO, LANES), f32), pltpu.VMEM((N_SLABS, tm, LANES), f32)],
        compiler_params=_params(2),
        name=f"conv_dw_l{layer}_{tag}",
    )(u, u, u, x, mods, dww, row(dw_b), row(ln_g), row(ln_b), pw2_w, row(pw2_b))


def _rope_tables(n_tok):
    n_freq = HEAD_DIM // 4
    pos = jnp.arange(n_tok)
    inv = ROPE_BASE ** (-jnp.arange(n_freq, dtype=f32) / n_freq)
    row = (pos // GRID_W).astype(f32)[:, None] * inv
    col = (pos % GRID_W).astype(f32)[:, None] * inv
    cos = jnp.concatenate([jnp.cos(row), jnp.cos(row), jnp.cos(col), jnp.cos(col)], axis=1)
    sin = jnp.concatenate([-jnp.sin(row), jnp.sin(row), -jnp.sin(col), jnp.sin(col)], axis=1)
    reps = LANES // HEAD_DIM
    return jnp.tile(cos, (1, reps)), jnp.tile(sin, (1, reps))


def _qkv_kernel(x_ref, mod_ref, g_ref, w_ref, cos_ref, sin_ref, q_ref, k_ref, v_ref):
    h, _ = _modulated(x_ref[0], mod_ref, g_ref, 1, _mod_row(False))
    qkv = jnp.dot(h, w_ref[...], preferred_element_type=f32)
    cos = cos_ref[...]
    sin = sin_ref[...]
    n_freq = HEAD_DIM // 4
    lane = lax.broadcasted_iota(jnp.int32, cos.shape, 1)
    is_x1 = (lane % (2 * n_freq)) < n_freq

    def rope(xb):
        partner = jnp.where(is_x1, pltpu.roll(xb, LANES - n_freq, 1), pltpu.roll(xb, n_freq, 1))
        return xb * cos + partner * sin

    for j in range(Q_DIM // LANES):
        sl = slice(j * LANES, (j + 1) * LANES)
        q_ref[0, :, sl] = (rope(qkv[:, sl]) * (HEAD_DIM ** -0.5)).astype(bf16)
    for j in range(KV_DIM // LANES):
        sl = slice(j * LANES, (j + 1) * LANES)
        k_ref[0, :, sl] = rope(qkv[:, Q_DIM + j * LANES:Q_DIM + (j + 1) * LANES]).astype(bf16)
    v_ref[0] = qkv[:, Q_DIM + KV_DIM:].astype(bf16)


def _kv_kernel(x_ref, mod_ref, g_ref, w_ref, k_ref, v_ref):
    h, _ = _modulated(x_ref[0], mod_ref, g_ref, 1, _mod_row(True))
    kv = jnp.dot(h, w_ref[...], preferred_element_type=f32)
    k_ref[0] = kv[:, :KV_DIM].astype(bf16)
    v_ref[0] = kv[:, KV_DIM:].astype(bf16)


def _attn_kernel(q_ref, kp_ref, kc_ref, kn_ref, vp_ref, vc_ref, vn_ref, kx_ref, vx_ref, sink_ref,
                 wo_ref, x_ref, mod_ref, o_ref, o_scr):
    n = pl.program_id(1)
    n_blocks = pl.num_programs(1)
    q = q_ref[0]
    k_win = jnp.concatenate([kp_ref[0], kc_ref[0], kn_ref[0]], axis=0)
    v_win = jnp.concatenate([vp_ref[0], vc_ref[0], vn_ref[0]], axis=0)
    k_ctx = kx_ref[0]
    v_ctx = vx_ref[0]

    rows = GROUP * Q_BLOCK
    shape = (rows, 3 * Q_BLOCK)
    q_pos = lax.broadcasted_iota(jnp.int32, shape, 0) % Q_BLOCK
    col = lax.broadcasted_iota(jnp.int32, shape, 1)
    rel = col - Q_BLOCK - q_pos
    in_band = jnp.where(jnp.abs(rel) <= WINDOW, 1, 0)
    lo = jnp.where(n > 0, 0, Q_BLOCK)
    hi = jnp.where(n < n_blocks - 1, 3 * Q_BLOCK, 2 * Q_BLOCK)
    valid = (in_band * jnp.where(col >= lo, 1, 0) * jnp.where(col < hi, 1, 0)) > 0

    contract_last = (((1,), (1,)), ((), ()))
    for h in range(N_KV_HEADS):
        heads = [h * GROUP + g for g in range(GROUP)]
        qs = jnp.concatenate([q[:, a * HEAD_DIM:(a + 1) * HEAD_DIM] for a in heads], axis=0)
        hs = slice(h * HEAD_DIM, (h + 1) * HEAD_DIM)
        s_win = lax.dot_general(qs, k_win[:, hs], contract_last, preferred_element_type=f32)
        s_win = jnp.where(valid, s_win, MASK_VALUE)
        s_ctx = lax.dot_general(qs, k_ctx[:, hs], contract_last, preferred_element_type=f32)
        sink = jnp.concatenate([jnp.full((Q_BLOCK, 1), sink_ref[a], f32) for a in heads], axis=0)
        m = jnp.maximum(jnp.maximum(jnp.max(s_win, axis=-1, keepdims=True),
                                    jnp.max(s_ctx, axis=-1, keepdims=True)), sink)
        e_win = jnp.exp(s_win - m)
        e_ctx = jnp.exp(s_ctx - m)
        denom = (jnp.sum(e_win, axis=-1, keepdims=True) + jnp.sum(e_ctx, axis=-1, keepdims=True)
                 + jnp.exp(sink - m))
        o = (jnp.dot(e_win.astype(bf16), v_win[:, hs], preferred_element_type=f32)
             + jnp.dot(e_ctx.astype(bf16), v_ctx[:, hs], preferred_element_type=f32)) / denom
        for g, a in enumerate(heads):
            o_scr[:, a * HEAD_DIM:(a + 1) * HEAD_DIM] = o[g * Q_BLOCK:(g + 1) * Q_BLOCK]

    y = jnp.dot(o_scr[...].astype(bf16), wo_ref[...], preferred_element_type=f32)
    gate = mod_ref[0, 5, pl.ds(pl.program_id(0), 1), :]
    o_ref[0] = x_ref[0] + gate * y


def _attn_mixer(x, ctx, mods, norm_g, w_qkv, w_o, sink, *, layer):
    batch, seq, _ = x.shape
    n_ctx = ctx.shape[1]
    assert seq % TOKEN_BLOCK == 0 and seq % Q_BLOCK == 0 and Q_BLOCK == WINDOW
    cos, sin = _rope_tables(seq)
    w_qkv_b = w_qkv.astype(bf16)
    tm = TOKEN_BLOCK
    q, k, v = pl.pallas_call(
        _qkv_kernel,
        grid=(batch, seq // tm),
        in_specs=[
            _tok_spec(tm), _mod_spec(layer), _g_spec(layer), _resident(w_qkv_b.shape),
            pl.BlockSpec((tm, LANES), lambda b, t: (t, 0)),
            pl.BlockSpec((tm, LANES), lambda b, t: (t, 0)),
        ],
        out_specs=[_tok_spec(tm, Q_DIM), _tok_spec(tm, KV_DIM), _tok_spec(tm, KV_DIM)],
        out_shape=[jax.ShapeDtypeStruct((batch, seq, Q_DIM), bf16),
                   jax.ShapeDtypeStruct((batch, seq, KV_DIM), bf16),
                   jax.ShapeDtypeStruct((batch, seq, KV_DIM), bf16)],
        compiler_params=_params(2),
        name=f"attn_qkv_l{layer}",
    )(x, mods, norm_g, w_qkv_b, cos, sin)

    w_kv_b = w_qkv_b[:, Q_DIM:]
    k_ctx, v_ctx = pl.pallas_call(
        _kv_kernel,
        grid=(batch, 1),
        in_specs=[_tok_spec(n_ctx), _mod_spec(layer), _g_spec(layer), _resident(w_kv_b.shape)],
        out_specs=[_tok_spec(n_ctx, KV_DIM), _tok_spec(n_ctx, KV_DIM)],
        out_shape=[jax.ShapeDtypeStruct((batch, n_ctx, KV_DIM), bf16)] * 2,
        compiler_params=_params(2),
        name=f"attn_kv_ctx_l{layer}",
    )(ctx, mods, norm_g, w_kv_b)

    n_blocks = seq // Q_BLOCK
    blk = lambda f: pl.BlockSpec((1, Q_BLOCK, KV_DIM), f)
    prev = blk(lambda b, n: (b, jnp.maximum(n - 1, 0), 0))
    cur = blk(lambda b, n: (b, n, 0))
    nxt = blk(lambda b, n: (b, jnp.minimum(n + 1, n_blocks - 1), 0))
    ctx_spec = pl.BlockSpec((1, n_ctx, KV_DIM), lambda b, n: (b, 0, 0))
    w_o_b = w_o.astype(bf16)
    return pl.pallas_call(
        _attn_kernel,
        grid=(batch, n_blocks),
        in_specs=[
            _tok_spec(Q_BLOCK, Q_DIM), prev, cur, nxt, prev, cur, nxt, ctx_spec, ctx_spec,
            pl.BlockSpec(memory_space=pltpu.SMEM),
            _resident(w_o_b.shape), _tok_spec(Q_BLOCK), _mod_spec(layer),
        ],
        out_specs=_tok_spec(Q_BLOCK),
        out_shape=jax.ShapeDtypeStruct(x.shape, f32),
        scratch_shapes=[pltpu.VMEM((Q_BLOCK, Q_DIM), f32)],
        compiler_params=_params(2),
        name=f"attn_core_l{layer}",
    )(q, k, k, k, v, v, v, k_ctx, v_ctx, sink, w_o_b, x, mods)


def kernel(x, c, ctx, c_ctx, norm_g, ada_w, ada_b, ffn1_wi, ffn1_wo, ffn2_wi, ffn2_wo, conv_pw1_w, conv_pw1_b, conv_dw_w, conv_dw_b, conv_ln_g, conv_ln_b, conv_pw2_w, conv_pw2_b, attn_w_qkv, attn_w_o, attn_sink, final_g):
    mods = _ada_table(c, c_ctx, ada_w, ada_b)
    for i in range(DEPTH):
        last = i == DEPTH - 1
        j = i // 2
        ffn = functools.partial(_ffn, mods=mods, norm_g=norm_g, final_g=final_g, layer=i)
        wi1, wo1 = _ffn_weights(ffn1_wi[i], ffn1_wo[i])
        wi2, wo2 = _ffn_weights(ffn2_wi[i], ffn2_wo[i])

        x = ffn(x, wi=wi1, wo=wo1, sub=0, is_ctx=False)
        ctx = ffn(ctx, wi=wi1, wo=wo1, sub=0, is_ctx=True)

        if i % 2 == 0:
            conv = functools.partial(
                _conv_mixer, mods=mods, norm_g=norm_g, pw1_w=conv_pw1_w[j].astype(bf16), pw1_b=conv_pw1_b[j],
                dw_w=conv_dw_w[j], dw_b=conv_dw_b[j], ln_g=conv_ln_g[j], ln_b=conv_ln_b[j],
                pw2_w=conv_pw2_w[j].astype(bf16), pw2_b=conv_pw2_b[j], layer=i)
            x = conv(x, is_ctx=False)
            if not last:
                ctx = conv(ctx, is_ctx=True)
        else:
            assert last, "context outputs of the attention mixer are only needed by a following layer"
            x = _attn_mixer(x, ctx, mods, norm_g, attn_w_qkv[j], attn_w_o[j], attn_sink[j], layer=i)

        x = ffn(x, wi=wi2, wo=wo2, sub=2, is_ctx=False, final=last)
        if not last:
            ctx = ffn(ctx, wi=wi2, wo=wo2, sub=2, is_ctx=True)
    return x
```

```python
import functools

import jax
import jax.numpy as jnp
from jax import lax
from jax.experimental import pallas as pl
from jax.experimental.pallas import tpu as pltpu

D_MODEL = 1024
D_FF = 2816
DEPTH = 2
GRID_W = 64
CONV_WIDTH = 31
CONV_PAD = (CONV_WIDTH - 1) // 2
N_Q_HEADS = 16
N_KV_HEADS = 4
GROUP = N_Q_HEADS // N_KV_HEADS
HEAD_DIM = 64
WINDOW = 128
ROPE_BASE = 10000.0
NORM_EPS = 1e-6
N_MOD = 9
Q_DIM = N_Q_HEADS * HEAD_DIM
KV_DIM = N_KV_HEADS * HEAD_DIM

LANES = 128
SUBLANES = 8
MOD_ROWS = 8
CTX_ROW = 4
FFN_CHUNK = 256
N_FFN_CHUNKS = D_FF // FFN_CHUNK
TOKEN_BLOCK = 512
HALO = 16
CONV_PHASES = 4
VREG_ROWS = CONV_PHASES * SUBLANES
CONV_ROWS = 8 * VREG_ROWS
CONV_PARTS = 2
N_SLABS = D_MODEL // LANES
LOG2E = 1.4426950408889634
Q_BLOCK = 128
MASK_VALUE = -1e30
VMEM_LIMIT_BYTES = 56 * 1024 * 1024

f32 = jnp.float32
bf16 = jnp.bfloat16


def _sigmoid(x):
    return 1.0 / (1.0 + jnp.exp(-x))


def _silu(x):
    return x * _sigmoid(x)


def _params(n_grid):
    return pltpu.CompilerParams(
        dimension_semantics=("arbitrary",) * n_grid,
        vmem_limit_bytes=VMEM_LIMIT_BYTES)


def _resident(shape):
    return pl.BlockSpec(shape, lambda *_: (0,) * len(shape), pipeline_mode=pl.Buffered(1))


def _modulated(x, mod_ref, g_ref, sub, row):
    shift = mod_ref[0, 3 * sub + 0, pl.ds(row, 1), :]
    scale = mod_ref[0, 3 * sub + 1, pl.ds(row, 1), :]
    gate = mod_ref[0, 3 * sub + 2, pl.ds(row, 1), :]
    g = g_ref[0, sub:sub + 1, :]
    y = x * lax.rsqrt(jnp.mean(x * x, axis=-1, keepdims=True) + NORM_EPS)
    h = (y * g) * (1.0 + scale) + shift
    return h.astype(bf16), gate


def _mod_row(is_ctx):
    return CTX_ROW if is_ctx else pl.program_id(0)


def _ada_kernel(c_ref, w_ref, b_ref, o_ref):
    s = _silu(c_ref[...]).astype(bf16)
    w = w_ref[0].astype(bf16)
    o_ref[0, 0] = jnp.dot(s, w, preferred_element_type=f32) + b_ref[0, 0]


def _ada_table(c, c_ctx, ada_w, ada_b):
    batch = c.shape[0]
    assert batch <= CTX_ROW < MOD_ROWS
    cvec = jnp.zeros((MOD_ROWS, D_MODEL), f32).at[:batch].set(c).at[CTX_ROW].set(c_ctx)
    bias = ada_b.reshape(DEPTH, N_MOD, 1, D_MODEL)
    return pl.pallas_call(
        _ada_kernel,
        grid=(DEPTH, N_MOD),
        in_specs=[
            pl.BlockSpec((MOD_ROWS, D_MODEL), lambda i, n: (0, 0)),
            pl.BlockSpec((1, D_MODEL, D_MODEL), lambda i, n: (i, 0, n)),
            pl.BlockSpec((1, 1, 1, D_MODEL), lambda i, n: (i, n, 0, 0)),
        ],
        out_specs=pl.BlockSpec((1, 1, MOD_ROWS, D_MODEL), lambda i, n: (i, n, 0, 0)),
        out_shape=jax.ShapeDtypeStruct((DEPTH, N_MOD, MOD_ROWS, D_MODEL), f32),
        compiler_params=_params(2),
        name="ada_table",
    )(cvec, ada_w, bias)


def _mod_spec(layer):
    return pl.BlockSpec((1, N_MOD, MOD_ROWS, D_MODEL), lambda *_: (layer, 0, 0, 0))


def _g_spec(layer):
    return pl.BlockSpec((1, 3, D_MODEL), lambda *_: (layer, 0, 0))


def _tok_spec(tm, width=D_MODEL):
    return pl.BlockSpec((1, tm, width), lambda b, t: (b, t, 0))


def _ffn_kernel(x_ref, mod_ref, g_ref, wi_ref, wo_ref, fg_ref, o_ref, *, sub, blocks_per_batch, final):
    x = x_ref[...]
    row = CTX_ROW if blocks_per_batch is None else pl.program_id(0) // blocks_per_batch
    h, gate = _modulated(x, mod_ref, g_ref, sub, row)
    acc = jnp.zeros(x.shape, f32)
    for c in range(N_FFN_CHUNKS):
        lo = c * FFN_CHUNK
        w_gu = jnp.concatenate([wi_ref[:, lo:lo + FFN_CHUNK], wi_ref[:, D_FF + lo:D_FF + lo + FFN_CHUNK]], axis=1)
        gu = jnp.dot(h, w_gu, preferred_element_type=f32)
        a = _silu(gu[:, :FFN_CHUNK]) * gu[:, FFN_CHUNK:]
        acc = acc + jnp.dot(a.astype(bf16), wo_ref[lo:lo + FFN_CHUNK, :], preferred_element_type=f32)
    out = x + 0.5 * gate * acc
    if final:
        out = out * lax.rsqrt(jnp.mean(out * out, axis=-1, keepdims=True) + NORM_EPS) * fg_ref[...]
    o_ref[...] = out


def _ffn(x, mods, norm_g, wi, wo, final_g, *, layer, sub, is_ctx, final=False):
    batch, length, _ = x.shape
    tm = TOKEN_BLOCK
    xf = x.reshape(batch * length, D_MODEL)
    assert xf.shape[0] % tm == 0 and (is_ctx or length % tm == 0)
    spec = pl.BlockSpec((tm, D_MODEL), lambda i: (i, 0))
    out = pl.pallas_call(
        functools.partial(_ffn_kernel, sub=sub, blocks_per_batch=None if is_ctx else length // tm, final=final),
        grid=(xf.shape[0] // tm,),
        in_specs=[spec, _mod_spec(layer), _g_spec(layer), _resident(wi.shape), _resident(wo.shape),
                  pl.BlockSpec((1, D_MODEL), lambda i: (0, 0))],
        out_specs=spec,
        out_shape=jax.ShapeDtypeStruct(xf.shape, f32),
        compiler_params=_params(1),
        name=f"ffn_l{layer}_s{sub}_{'ctx' if is_ctx else 'lat'}",
    )(xf, mods, norm_g, wi, wo, final_g.reshape(1, D_MODEL))
    return out.reshape(x.shape)


def _glu_kernel(x_ref, mod_ref, g_ref, w_ref, b_ref, u_ref, *, is_ctx):
    h, _ = _modulated(x_ref[0], mod_ref, g_ref, 1, _mod_row(is_ctx))
    a = jnp.dot(h, w_ref[...], preferred_element_type=f32) + b_ref[...]
    u_ref[0] = a[:, :D_MODEL] * _sigmoid(a[:, D_MODEL:])


def _conv_kernel(um_ref, up_ref, un_ref, x_ref, mod_ref, dww_ref, dwb_ref, lng_ref, lnb_ref,
                 w2_ref, b2_ref, o_ref, buf, conv, *, is_ctx):
    t = pl.program_id(1)
    tm = um_ref.shape[1]
    for j in range(N_SLABS):
        sl = slice(j * LANES, (j + 1) * LANES)
        buf[j, 0:HALO, :] = jnp.where(t > 0, up_ref[0, :, sl], 0.0)
        buf[j, HALO:HALO + tm, :] = um_ref[0, :, sl]
        buf[j, HALO + tm:, :] = jnp.where(t < pl.num_programs(1) - 1, un_ref[0, :, sl], 0.0)

    def step(i, carry):
        j = i % N_SLABS
        base = (i // N_SLABS) * CONV_ROWS
        outs = [(g, p) for g in range(CONV_ROWS // VREG_ROWS) for p in range(CONV_PHASES)]
        parts = {o: [None] * CONV_PARTS for o in outs}
        taps = {}
        for k in range(CONV_WIDTH):
            wk = dww_ref[j, k * SUBLANES:(k + 1) * SUBLANES, :]
            for g, p in outs:
                q = g * VREG_ROWS + p + k - CONV_PAD
                if q not in taps:
                    taps[q] = buf[j, pl.ds(base + HALO + q, SUBLANES, stride=CONV_PHASES), :]
                term = taps[q] * wk
                prev = parts[g, p][k % CONV_PARTS]
                parts[g, p][k % CONV_PARTS] = term if prev is None else prev + term
        for g, p in outs:
            total = parts[g, p][0]
            for part in parts[g, p][1:]:
                total = total + part
            conv[j, pl.ds(base + g * VREG_ROWS + p, SUBLANES, stride=CONV_PHASES), :] = total
        return carry

    lax.fori_loop(0, N_SLABS * (tm // CONV_ROWS), step, 0)

    u = jnp.concatenate([conv[j] for j in range(N_SLABS)], axis=1) + dwb_ref[...]
    mu = jnp.mean(u, axis=-1, keepdims=True)
    uc = u - mu
    var = jnp.mean(uc * uc, axis=-1, keepdims=True)
    un = (uc * lax.rsqrt(var + NORM_EPS)) * lng_ref[...] + lnb_ref[...]
    y = jnp.dot(_silu(un).astype(bf16), w2_ref[...], preferred_element_type=f32) + b2_ref[...]
    gate = mod_ref[0, 5, pl.ds(_mod_row(is_ctx), 1), :]
    o_ref[0] = x_ref[0] + gate * y


def _conv_mixer(x, mods, norm_g, pw1_w, pw1_b, dw_w, dw_b, ln_g, ln_b, pw2_w, pw2_b, *, layer, is_ctx):
    batch, length, _ = x.shape
    tm = min(TOKEN_BLOCK, length)
    assert length % tm == 0 and tm % HALO == 0 and tm % CONV_ROWS == 0
    tag = "ctx" if is_ctx else "lat"
    row = lambda v: v.reshape(1, -1)
    u = pl.pallas_call(
        functools.partial(_glu_kernel, is_ctx=is_ctx),
        grid=(batch, length // tm),
        in_specs=[
            _tok_spec(tm), _mod_spec(layer), _g_spec(layer),
            _resident(pw1_w.shape), pl.BlockSpec((1, 2 * D_MODEL), lambda b, t: (0, 0)),
        ],
        out_specs=_tok_spec(tm),
        out_shape=jax.ShapeDtypeStruct(x.shape, f32),
        compiler_params=_params(2),
        name=f"conv_glu_l{layer}_{tag}",
    )(x, mods, norm_g, pw1_w, row(pw1_b))

    halo_per_block = tm // HALO
    n_halo = length // HALO
    vec = pl.BlockSpec((1, D_MODEL), lambda b, t: (0, 0))
    dww = jnp.repeat(dw_w.reshape(CONV_WIDTH, N_SLABS, LANES).transpose(1, 0, 2), SUBLANES, axis=1)
    return pl.pallas_call(
        functools.partial(_conv_kernel, is_ctx=is_ctx),
        grid=(batch, length // tm),
        in_specs=[
            _tok_spec(tm),
            pl.BlockSpec((1, HALO, D_MODEL), lambda b, t: (b, jnp.maximum(t * halo_per_block - 1, 0), 0)),
            pl.BlockSpec((1, HALO, D_MODEL),
                         lambda b, t: (b, jnp.minimum((t + 1) * halo_per_block, n_halo - 1), 0)),
            _tok_spec(tm), _mod_spec(layer),
            pl.BlockSpec(dww.shape, lambda b, t: (0, 0, 0)),
            vec, vec, vec, _resident(pw2_w.shape), vec,
        ],
        out_specs=_tok_spec(tm),
        out_shape=jax.ShapeDtypeStruct(x.shape, f32),
        scratch_shapes=[pltpu.VMEM((N_SLABS, tm + 2 * HALO, LANES), f32), pltpu.VMEM((N_SLABS, tm, LANES), f32)],
        compiler_params=_params(2),
        name=f"conv_dw_l{layer}_{tag}",
    )(u, u, u, x, mods, dww, row(dw_b), row(ln_g), row(ln_b), pw2_w, row(pw2_b))


def _rope_tables(n_tok):
    n_freq = HEAD_DIM // 4
    pos = jnp.arange(n_tok)
    inv = ROPE_BASE ** (-jnp.arange(n_freq, dtype=f32) / n_freq)
    row = (pos // GRID_W).astype(f32)[:, None] * inv
    col = (pos % GRID_W).astype(f32)[:, None] * inv
    cos = jnp.concatenate([jnp.cos(row), jnp.cos(row), jnp.cos(col), jnp.cos(col)], axis=1)
    sin = jnp.concatenate([-jnp.sin(row), jnp.sin(row), -jnp.sin(col), jnp.sin(col)], axis=1)
    reps = LANES // HEAD_DIM
    return jnp.tile(cos, (1, reps)), jnp.tile(sin, (1, reps))


def _qkv_kernel(x_ref, mod_ref, g_ref, w_ref, cos_ref, sin_ref, q_ref, k_ref, v_ref):
    h, _ = _modulated(x_ref[0], mod_ref, g_ref, 1, _mod_row(False))
    qkv = jnp.dot(h, w_ref[...], preferred_element_type=f32)
    cos = cos_ref[...]
    sin = sin_ref[...]
    n_freq = HEAD_DIM // 4
    lane = lax.broadcasted_iota(jnp.int32, cos.shape, 1)
    is_x1 = (lane % (2 * n_freq)) < n_freq

    def rope(xb):
        partner = jnp.where(is_x1, pltpu.roll(xb, LANES - n_freq, 1), pltpu.roll(xb, n_freq, 1))
        return xb * cos + partner * sin

    for j in range(Q_DIM // LANES):
        sl = slice(j * LANES, (j + 1) * LANES)
        q_ref[0, :, sl] = (rope(qkv[:, sl]) * (HEAD_DIM ** -0.5 * LOG2E)).astype(bf16)
    for j in range(KV_DIM // LANES):
        sl = slice(j * LANES, (j + 1) * LANES)
        k_ref[0, :, sl] = rope(qkv[:, Q_DIM + j * LANES:Q_DIM + (j + 1) * LANES]).astype(bf16)
    v_ref[0] = qkv[:, Q_DIM + KV_DIM:].astype(bf16)


def _kv_kernel(x_ref, mod_ref, g_ref, w_ref, k_ref, v_ref):
    h, _ = _modulated(x_ref[0], mod_ref, g_ref, 1, _mod_row(True))
    kv = jnp.dot(h, w_ref[...], preferred_element_type=f32)
    k_ref[0] = kv[:, :KV_DIM].astype(bf16)
    v_ref[0] = kv[:, KV_DIM:].astype(bf16)


def _attn_kernel(q_ref, kp_ref, kc_ref, kn_ref, vp_ref, vc_ref, vn_ref, kx_ref, vx_ref, sink_ref,
                 wo_ref, x_ref, mod_ref, o_ref, o_scr):
    n = pl.program_id(1)
    n_blocks = pl.num_programs(1)
    q = q_ref[0]
    k_all = jnp.concatenate([kp_ref[0], kc_ref[0], kn_ref[0], kx_ref[0]], axis=0)
    v_all = jnp.concatenate([vp_ref[0], vc_ref[0], vn_ref[0], vx_ref[0]], axis=0)

    cols = GROUP * Q_BLOCK
    key = lax.broadcasted_iota(jnp.int32, (Q_BLOCK, cols), 0)
    qry = lax.broadcasted_iota(jnp.int32, (Q_BLOCK, cols), 1) % Q_BLOCK
    bias_prev = jnp.where((key >= qry) & (n > 0), 0.0, MASK_VALUE)
    bias_next = jnp.where((key <= qry) & (n < n_blocks - 1), 0.0, MASK_VALUE)

    contract_last = (((1,), (1,)), ((), ()))
    contract_first = (((0,), (0,)), ((), ()))
    for h in range(N_KV_HEADS):
        heads = [h * GROUP + g for g in range(GROUP)]
        qs = jnp.concatenate([q[:, a * HEAD_DIM:(a + 1) * HEAD_DIM] for a in heads], axis=0)
        hs = slice(h * HEAD_DIM, (h + 1) * HEAD_DIM)
        s = lax.dot_general(k_all[:, hs], qs, contract_last, preferred_element_type=f32)
        s_prev = s[0:Q_BLOCK] + bias_prev
        s_cur = s[Q_BLOCK:2 * Q_BLOCK]
        s_next = s[2 * Q_BLOCK:3 * Q_BLOCK] + bias_next
        s_ctx = s[3 * Q_BLOCK:]
        sink = sink_ref[h] * LOG2E
        m = jnp.maximum(
            jnp.maximum(jnp.max(s_prev, axis=0, keepdims=True), jnp.max(s_cur, axis=0, keepdims=True)),
            jnp.maximum(jnp.max(s_next, axis=0, keepdims=True), jnp.max(s_ctx, axis=0, keepdims=True)))
        m = jnp.maximum(m, sink)
        e = jnp.exp2(jnp.concatenate([s_prev, s_cur, s_next, s_ctx], axis=0) - m)
        denom = jnp.sum(e, axis=0, keepdims=True) + jnp.exp2(sink - m)
        o_t = lax.dot_general(v_all[:, hs], e.astype(bf16), contract_first, preferred_element_type=f32)
        o_t = o_t * (1.0 / denom)
        for pair in range(GROUP // 2):
            g = 2 * pair
            two = jnp.concatenate([o_t[:, g * Q_BLOCK:(g + 1) * Q_BLOCK],
                                   o_t[:, (g + 1) * Q_BLOCK:(g + 2) * Q_BLOCK]], axis=0)
            o_scr[:, heads[g] * HEAD_DIM:(heads[g] + 2) * HEAD_DIM] = two.T

    y = jnp.dot(o_scr[...].astype(bf16), wo_ref[...], preferred_element_type=f32)
    gate = mod_ref[0, 5, pl.ds(pl.program_id(0), 1), :]
    o_ref[0] = x_ref[0] + gate * y


def _attn_mixer(x, ctx, mods, norm_g, w_qkv, w_o, sink, *, layer):
    batch, seq, _ = x.shape
    n_ctx = ctx.shape[1]
    assert seq % TOKEN_BLOCK == 0 and seq % Q_BLOCK == 0 and Q_BLOCK == WINDOW
    cos, sin = _rope_tables(seq)
    w_qkv_b = w_qkv.astype(bf16)
    tm = TOKEN_BLOCK
    q, k, v = pl.pallas_call(
        _qkv_kernel,
        grid=(batch, seq // tm),
        in_specs=[
            _tok_spec(tm), _mod_spec(layer), _g_spec(layer), _resident(w_qkv_b.shape),
            pl.BlockSpec((tm, LANES), lambda b, t: (t, 0)),
            pl.BlockSpec((tm, LANES), lambda b, t: (t, 0)),
        ],
        out_specs=[_tok_spec(tm, Q_DIM), _tok_spec(tm, KV_DIM), _tok_spec(tm, KV_DIM)],
        out_shape=[jax.ShapeDtypeStruct((batch, seq, Q_DIM), bf16),
                   jax.ShapeDtypeStruct((batch, seq, KV_DIM), bf16),
                   jax.ShapeDtypeStruct((batch, seq, KV_DIM), bf16)],
        compiler_params=_params(2),
        name=f"attn_qkv_l{layer}",
    )(x, mods, norm_g, w_qkv_b, cos, sin)

    w_kv_b = w_qkv_b[:, Q_DIM:]
    k_ctx, v_ctx = pl.pallas_call(
        _kv_kernel,
        grid=(batch, 1),
        in_specs=[_tok_spec(n_ctx), _mod_spec(layer), _g_spec(layer), _resident(w_kv_b.shape)],
        out_specs=[_tok_spec(n_ctx, KV_DIM), _tok_spec(n_ctx, KV_DIM)],
        out_shape=[jax.ShapeDtypeStruct((batch, n_ctx, KV_DIM), bf16)] * 2,
        compiler_params=_params(2),
        name=f"attn_kv_ctx_l{layer}",
    )(ctx, mods, norm_g, w_kv_b)

    n_blocks = seq // Q_BLOCK
    blk = lambda f: pl.BlockSpec((1, Q_BLOCK, KV_DIM), f)
    prev = blk(lambda b, n: (b, jnp.maximum(n - 1, 0), 0))
    cur = blk(lambda b, n: (b, n, 0))
    nxt = blk(lambda b, n: (b, jnp.minimum(n + 1, n_blocks - 1), 0))
    ctx_spec = pl.BlockSpec((1, n_ctx, KV_DIM), lambda b, n: (b, 0, 0))
    w_o_b = w_o.astype(bf16)
    sink_cols = jnp.repeat(sink.reshape(N_KV_HEADS, GROUP), Q_BLOCK, axis=1).reshape(N_KV_HEADS, 1, GROUP * Q_BLOCK)
    return pl.pallas_call(
        _attn_kernel,
        grid=(batch, n_blocks),
        in_specs=[
            _tok_spec(Q_BLOCK, Q_DIM), prev, cur, nxt, prev, cur, nxt, ctx_spec, ctx_spec,
            pl.BlockSpec(sink_cols.shape, lambda b, n: (0, 0, 0)),
            _resident(w_o_b.shape), _tok_spec(Q_BLOCK), _mod_spec(layer),
        ],
        out_specs=_tok_spec(Q_BLOCK),
        out_shape=jax.ShapeDtypeStruct(x.shape, f32),
        scratch_shapes=[pltpu.VMEM((Q_BLOCK, Q_DIM), f32)],
        compiler_params=_params(2),
        name=f"attn_core_l{layer}",
    )(q, k, k, k, v, v, v, k_ctx, v_ctx, sink_cols, w_o_b, x, mods)


def kernel(x, c, ctx, c_ctx, norm_g, ada_w, ada_b, ffn1_wi, ffn1_wo, ffn2_wi, ffn2_wo, conv_pw1_w, conv_pw1_b, conv_dw_w, conv_dw_b, conv_ln_g, conv_ln_b, conv_pw2_w, conv_pw2_b, attn_w_qkv, attn_w_o, attn_sink, final_g):
    mods = _ada_table(c, c_ctx, ada_w, ada_b)
    for i in range(DEPTH):
        last = i == DEPTH - 1
        j = i // 2
        ffn = functools.partial(_ffn, mods=mods, norm_g=norm_g, final_g=final_g, layer=i)
        wi1, wo1 = ffn1_wi[i].astype(bf16), ffn1_wo[i].astype(bf16)
        wi2, wo2 = ffn2_wi[i].astype(bf16), ffn2_wo[i].astype(bf16)

        x = ffn(x, wi=wi1, wo=wo1, sub=0, is_ctx=False)
        ctx = ffn(ctx, wi=wi1, wo=wo1, sub=0, is_ctx=True)

        if i % 2 == 0:
            conv = functools.partial(
                _conv_mixer, mods=mods, norm_g=norm_g, pw1_w=conv_pw1_w[j].astype(bf16), pw1_b=conv_pw1_b[j],
                dw_w=conv_dw_w[j], dw_b=conv_dw_b[j], ln_g=conv_ln_g[j], ln_b=conv_ln_b[j],
                pw2_w=conv_pw2_w[j].astype(bf16), pw2_b=conv_pw2_b[j], layer=i)
            x = conv(x, is_ctx=False)
            if not last:
                ctx = conv(ctx, is_ctx=True)
        else:
            assert last, "context outputs of the attention mixer are only needed by a following layer"
            x = _attn_mixer(x, ctx, mods, norm_g, attn_w_qkv[j], attn_w_o[j], attn_sink[j], layer=i)

        x = ffn(x, wi=wi2, wo=wo2, sub=2, is_ctx=False, final=last)
        if not last:
            ctx = ffn(ctx, wi=wi2, wo=wo2, sub=2, is_ctx=True)
    return x
```

```python
import functools

import jax
import jax.numpy as jnp
from jax import lax
from jax.experimental import pallas as pl
from jax.experimental.pallas import tpu as pltpu

D_MODEL = 1024
D_FF = 2816
DEPTH = 2
GRID_W = 64
CONV_WIDTH = 31
CONV_PAD = (CONV_WIDTH - 1) // 2
N_Q_HEADS = 16
N_KV_HEADS = 4
GROUP = N_Q_HEADS // N_KV_HEADS
HEAD_DIM = 64
WINDOW = 128
ROPE_BASE = 10000.0
NORM_EPS = 1e-6
N_MOD = 9
Q_DIM = N_Q_HEADS * HEAD_DIM
KV_DIM = N_KV_HEADS * HEAD_DIM

LANES = 128
SUBLANES = 8
MOD_ROWS = 8
CTX_ROW = 4
FFN_CHUNK = 256
N_FFN_CHUNKS = D_FF // FFN_CHUNK
TOKEN_BLOCK = 512
HALO = 16
CONV_PHASES = 4
VREG_ROWS = CONV_PHASES * SUBLANES
CONV_ROWS = 8 * VREG_ROWS
CONV_PARTS = 2
N_SLABS = D_MODEL // LANES
LOG2E = 1.4426950408889634
Q_BLOCK = 128
Q_BLOCKS_PER_STEP = 4
Q_STEP = Q_BLOCK * Q_BLOCKS_PER_STEP
FFN_TOKEN_BLOCK = 1024
CAST_ROW_BLOCKS = 4
MASK_VALUE = -1e30
VMEM_LIMIT_BYTES = 56 * 1024 * 1024

f32 = jnp.float32
bf16 = jnp.bfloat16


def _sigmoid(x):
    return 1.0 / (1.0 + jnp.exp(-x))


def _silu(x):
    return x * _sigmoid(x)


def _params(n_grid):
    return pltpu.CompilerParams(
        dimension_semantics=("arbitrary",) * n_grid,
        vmem_limit_bytes=VMEM_LIMIT_BYTES)


def _resident(stack, index):
    return pl.BlockSpec((None,) + stack.shape[1:], lambda *_: (index, 0, 0), pipeline_mode=pl.Buffered(1))


def _cast_kernel(w_ref, o_ref):
    o_ref[...] = w_ref[...].astype(bf16)


def _to_bf16(w):
    n, rows, width = w.shape
    assert rows % (CAST_ROW_BLOCKS * 2 * SUBLANES) == 0
    spec = pl.BlockSpec((1, rows // CAST_ROW_BLOCKS, width), lambda l, r: (l, r, 0))
    return pl.pallas_call(
        _cast_kernel,
        grid=(n, CAST_ROW_BLOCKS),
        in_specs=[spec],
        out_specs=spec,
        out_shape=jax.ShapeDtypeStruct(w.shape, bf16),
        compiler_params=_params(2),
        name="cast_bf16",
    )(w)


def _modulated(x, mod_ref, g_ref, sub, row):
    shift = mod_ref[0, 3 * sub + 0, pl.ds(row, 1), :]
    scale = mod_ref[0, 3 * sub + 1, pl.ds(row, 1), :]
    gate = mod_ref[0, 3 * sub + 2, pl.ds(row, 1), :]
    g = g_ref[0, sub:sub + 1, :]
    y = x * lax.rsqrt(jnp.mean(x * x, axis=-1, keepdims=True) + NORM_EPS)
    h = (y * g) * (1.0 + scale) + shift
    return h.astype(bf16), gate


def _mod_row(is_ctx):
    return CTX_ROW if is_ctx else pl.program_id(0)


def _ada_kernel(c_ref, w_ref, b_ref, o_ref):
    s = _silu(c_ref[...]).astype(bf16)
    w = w_ref[0].astype(bf16)
    o_ref[0, 0] = jnp.dot(s, w, preferred_element_type=f32) + b_ref[0, 0]


def _ada_table(c, c_ctx, ada_w, ada_b):
    batch = c.shape[0]
    assert batch <= CTX_ROW < MOD_ROWS
    cvec = jnp.zeros((MOD_ROWS, D_MODEL), f32).at[:batch].set(c).at[CTX_ROW].set(c_ctx)
    bias = ada_b.reshape(DEPTH, N_MOD, 1, D_MODEL)
    return pl.pallas_call(
        _ada_kernel,
        grid=(DEPTH, N_MOD),
        in_specs=[
            pl.BlockSpec((MOD_ROWS, D_MODEL), lambda i, n: (0, 0)),
            pl.BlockSpec((1, D_MODEL, D_MODEL), lambda i, n: (i, 0, n)),
            pl.BlockSpec((1, 1, 1, D_MODEL), lambda i, n: (i, n, 0, 0)),
        ],
        out_specs=pl.BlockSpec((1, 1, MOD_ROWS, D_MODEL), lambda i, n: (i, n, 0, 0)),
        out_shape=jax.ShapeDtypeStruct((DEPTH, N_MOD, MOD_ROWS, D_MODEL), f32),
        compiler_params=_params(2),
        name="ada_table",
    )(cvec, ada_w, bias)


def _mod_spec(layer):
    return pl.BlockSpec((1, N_MOD, MOD_ROWS, D_MODEL), lambda *_: (layer, 0, 0, 0))


def _g_spec(layer):
    return pl.BlockSpec((1, 3, D_MODEL), lambda *_: (layer, 0, 0))


def _tok_spec(tm, width=D_MODEL):
    return pl.BlockSpec((1, tm, width), lambda b, t: (b, t, 0))


def _ffn_kernel(x_ref, mod_ref, g_ref, wi_ref, wo_ref, fg_ref, o_ref, *, sub, blocks_per_batch, final):
    x = x_ref[...]
    row = CTX_ROW if blocks_per_batch is None else pl.program_id(0) // blocks_per_batch
    h, gate = _modulated(x, mod_ref, g_ref, sub, row)
    acc = jnp.zeros(x.shape, f32)
    for c in range(N_FFN_CHUNKS):
        lo = c * FFN_CHUNK
        w_gu = jnp.concatenate([wi_ref[:, lo:lo + FFN_CHUNK], wi_ref[:, D_FF + lo:D_FF + lo + FFN_CHUNK]], axis=1)
        gu = jnp.dot(h, w_gu, preferred_element_type=f32)
        a = _silu(gu[:, :FFN_CHUNK]) * gu[:, FFN_CHUNK:]
        acc = acc + jnp.dot(a.astype(bf16), wo_ref[lo:lo + FFN_CHUNK, :], preferred_element_type=f32)
    out = x + 0.5 * gate * acc
    if final:
        out = out * lax.rsqrt(jnp.mean(out * out, axis=-1, keepdims=True) + NORM_EPS) * fg_ref[...]
    o_ref[...] = out


def _ffn(x, mods, norm_g, wi, wo, final_g, *, layer, sub, is_ctx, final=False):
    batch, length, _ = x.shape
    tm = FFN_TOKEN_BLOCK
    xf = x.reshape(batch * length, D_MODEL)
    assert xf.shape[0] % tm == 0 and (is_ctx or length % tm == 0)
    spec = pl.BlockSpec((tm, D_MODEL), lambda i: (i, 0))
    out = pl.pallas_call(
        functools.partial(_ffn_kernel, sub=sub, blocks_per_batch=None if is_ctx else length // tm, final=final),
        grid=(xf.shape[0] // tm,),
        in_specs=[spec, _mod_spec(layer), _g_spec(layer), _resident(wi, layer), _resident(wo, layer),
                  pl.BlockSpec((1, D_MODEL), lambda i: (0, 0))],
        out_specs=spec,
        out_shape=jax.ShapeDtypeStruct(xf.shape, f32),
        compiler_params=_params(1),
        name=f"ffn_l{layer}_s{sub}_{'ctx' if is_ctx else 'lat'}",
    )(xf, mods, norm_g, wi, wo, final_g.reshape(1, D_MODEL))
    return out.reshape(x.shape)


def _glu_kernel(x_ref, mod_ref, g_ref, w_ref, b_ref, u_ref, *, is_ctx):
    h, _ = _modulated(x_ref[0], mod_ref, g_ref, 1, _mod_row(is_ctx))
    a = jnp.dot(h, w_ref[...], preferred_element_type=f32) + b_ref[...]
    u_ref[0] = a[:, :D_MODEL] * _sigmoid(a[:, D_MODEL:])


def _conv_kernel(um_ref, up_ref, un_ref, x_ref, mod_ref, dww_ref, dwb_ref, lng_ref, lnb_ref,
                 w2_ref, b2_ref, o_ref, buf, conv, *, is_ctx):
    t = pl.program_id(1)
    tm = um_ref.shape[1]
    for j in range(N_SLABS):
        sl = slice(j * LANES, (j + 1) * LANES)
        buf[j, 0:HALO, :] = jnp.where(t > 0, up_ref[0, :, sl], 0.0)
        buf[j, HALO:HALO + tm, :] = um_ref[0, :, sl]
        buf[j, HALO + tm:, :] = jnp.where(t < pl.num_programs(1) - 1, un_ref[0, :, sl], 0.0)

    def step(i, carry):
        j = i % N_SLABS
        base = (i // N_SLABS) * CONV_ROWS
        outs = [(g, p) for g in range(CONV_ROWS // VREG_ROWS) for p in range(CONV_PHASES)]
        parts = {o: [None] * CONV_PARTS for o in outs}
        taps = {}
        for k in range(CONV_WIDTH):
            wk = dww_ref[j, k * SUBLANES:(k + 1) * SUBLANES, :]
            for g, p in outs:
                q = g * VREG_ROWS + p + k - CONV_PAD
                if q not in taps:
                    taps[q] = buf[j, pl.ds(base + HALO + q, SUBLANES, stride=CONV_PHASES), :]
                term = taps[q] * wk
                prev = parts[g, p][k % CONV_PARTS]
                parts[g, p][k % CONV_PARTS] = term if prev is None else prev + term
        for g, p in outs:
            total = parts[g, p][0]
            for part in parts[g, p][1:]:
                total = total + part
            conv[j, pl.ds(base + g * VREG_ROWS + p, SUBLANES, stride=CONV_PHASES), :] = total
        return carry

    lax.fori_loop(0, N_SLABS * (tm // CONV_ROWS), step, 0)

    u = jnp.concatenate([conv[j] for j in range(N_SLABS)], axis=1) + dwb_ref[...]
    mu = jnp.mean(u, axis=-1, keepdims=True)
    uc = u - mu
    var = jnp.mean(uc * uc, axis=-1, keepdims=True)
    un = (uc * lax.rsqrt(var + NORM_EPS)) * lng_ref[...] + lnb_ref[...]
    y = jnp.dot(_silu(un).astype(bf16), w2_ref[...], preferred_element_type=f32) + b2_ref[...]
    gate = mod_ref[0, 5, pl.ds(_mod_row(is_ctx), 1), :]
    o_ref[0] = x_ref[0] + gate * y


def _conv_mixer(x, mods, norm_g, pw1_w, pw1_b, dw_w, dw_b, ln_g, ln_b, pw2_w, pw2_b, *, layer, member, is_ctx):
    batch, length, _ = x.shape
    tm = min(TOKEN_BLOCK, length)
    assert length % tm == 0 and tm % HALO == 0 and tm % CONV_ROWS == 0
    tag = "ctx" if is_ctx else "lat"
    row = lambda v: v.reshape(1, -1)
    u = pl.pallas_call(
        functools.partial(_glu_kernel, is_ctx=is_ctx),
        grid=(batch, length // tm),
        in_specs=[
            _tok_spec(tm), _mod_spec(layer), _g_spec(layer),
            _resident(pw1_w, member), pl.BlockSpec((1, 2 * D_MODEL), lambda b, t: (0, 0)),
        ],
        out_specs=_tok_spec(tm),
        out_shape=jax.ShapeDtypeStruct(x.shape, f32),
        compiler_params=_params(2),
        name=f"conv_glu_l{layer}_{tag}",
    )(x, mods, norm_g, pw1_w, row(pw1_b))

    halo_per_block = tm // HALO
    n_halo = length // HALO
    vec = pl.BlockSpec((1, D_MODEL), lambda b, t: (0, 0))
    dww = jnp.repeat(dw_w.reshape(CONV_WIDTH, N_SLABS, LANES).transpose(1, 0, 2), SUBLANES, axis=1)
    return pl.pallas_call(
        functools.partial(_conv_kernel, is_ctx=is_ctx),
        grid=(batch, length // tm),
        in_specs=[
            _tok_spec(tm),
            pl.BlockSpec((1, HALO, D_MODEL), lambda b, t: (b, jnp.maximum(t * halo_per_block - 1, 0), 0)),
            pl.BlockSpec((1, HALO, D_MODEL),
                         lambda b, t: (b, jnp.minimum((t + 1) * halo_per_block, n_halo - 1), 0)),
            _tok_spec(tm), _mod_spec(layer),
            pl.BlockSpec(dww.shape, lambda b, t: (0, 0, 0)),
            vec, vec, vec, _resident(pw2_w, member), vec,
        ],
        out_specs=_tok_spec(tm),
        out_shape=jax.ShapeDtypeStruct(x.shape, f32),
        scratch_shapes=[pltpu.VMEM((N_SLABS, tm + 2 * HALO, LANES), f32), pltpu.VMEM((N_SLABS, tm, LANES), f32)],
        compiler_params=_params(2),
        name=f"conv_dw_l{layer}_{tag}",
    )(u, u, u, x, mods, dww, row(dw_b), row(ln_g), row(ln_b), pw2_w, row(pw2_b))


def _rope_tables(n_tok):
    n_freq = HEAD_DIM // 4
    pos = jnp.arange(n_tok)
    inv = ROPE_BASE ** (-jnp.arange(n_freq, dtype=f32) / n_freq)
    row = (pos // GRID_W).astype(f32)[:, None] * inv
    col = (pos % GRID_W).astype(f32)[:, None] * inv
    cos = jnp.concatenate([jnp.cos(row), jnp.cos(row), jnp.cos(col), jnp.cos(col)], axis=1)
    sin = jnp.concatenate([-jnp.sin(row), jnp.sin(row), -jnp.sin(col), jnp.sin(col)], axis=1)
    reps = LANES // HEAD_DIM
    return jnp.tile(cos, (1, reps)), jnp.tile(sin, (1, reps))


def _qkv_kernel(x_ref, mod_ref, g_ref, w_ref, cos_ref, sin_ref, q_ref, k_ref, v_ref):
    h, _ = _modulated(x_ref[0], mod_ref, g_ref, 1, _mod_row(False))
    qkv = jnp.dot(h, w_ref[...], preferred_element_type=f32)
    cos = cos_ref[...]
    sin = sin_ref[...]
    n_freq = HEAD_DIM // 4
    lane = lax.broadcasted_iota(jnp.int32, cos.shape, 1)
    is_x1 = (lane % (2 * n_freq)) < n_freq

    def rope(xb):
        partner = jnp.where(is_x1, pltpu.roll(xb, LANES - n_freq, 1), pltpu.roll(xb, n_freq, 1))
        return xb * cos + partner * sin

    for j in range(Q_DIM // LANES):
        sl = slice(j * LANES, (j + 1) * LANES)
        q_ref[0, :, sl] = (rope(qkv[:, sl]) * (HEAD_DIM ** -0.5 * LOG2E)).astype(bf16)
    for j in range(KV_DIM // LANES):
        sl = slice(j * LANES, (j + 1) * LANES)
        k_ref[0, :, sl] = rope(qkv[:, Q_DIM + j * LANES:Q_DIM + (j + 1) * LANES]).astype(bf16)
    v_ref[0] = qkv[:, Q_DIM + KV_DIM:].astype(bf16)


def _kv_kernel(x_ref, mod_ref, g_ref, w_ref, k_ref, v_ref):
    h, _ = _modulated(x_ref[0], mod_ref, g_ref, 1, _mod_row(True))
    kv = jnp.dot(h, w_ref[:, Q_DIM:], preferred_element_type=f32)
    k_ref[0] = kv[:, :KV_DIM].astype(bf16)
    v_ref[0] = kv[:, KV_DIM:].astype(bf16)


def _attn_kernel(q_ref, kp_ref, kc_ref, kn_ref, vp_ref, vc_ref, vn_ref, kx_ref, vx_ref, sink_ref,
                 wo_ref, x_ref, mod_ref, o_ref, o_scr):
    n = pl.program_id(1)
    n_steps = pl.num_programs(1)
    k_ext = jnp.concatenate([kp_ref[0], kc_ref[0], kn_ref[0]], axis=0)
    v_ext = jnp.concatenate([vp_ref[0], vc_ref[0], vn_ref[0]], axis=0)
    k_ctx = kx_ref[0]
    v_ctx = vx_ref[0]

    cols = GROUP * Q_BLOCK
    key = lax.broadcasted_iota(jnp.int32, (Q_BLOCK, cols), 0)
    qry = lax.broadcasted_iota(jnp.int32, (Q_BLOCK, cols), 1) % Q_BLOCK
    tri_prev = jnp.where(key >= qry, 0.0, MASK_VALUE)
    tri_next = jnp.where(key <= qry, 0.0, MASK_VALUE)
    edge_prev = jnp.where(n > 0, tri_prev, MASK_VALUE)
    edge_next = jnp.where(n < n_steps - 1, tri_next, MASK_VALUE)

    contract_last = (((1,), (1,)), ((), ()))
    contract_first = (((0,), (0,)), ((), ()))
    units = [(b, h) for b in range(Q_BLOCKS_PER_STEP) for h in range(N_KV_HEADS)]

    def scores(b, h):
        q = q_ref[0, b * Q_BLOCK:(b + 1) * Q_BLOCK, :]
        qs = jnp.concatenate([q[:, a * HEAD_DIM:(a + 1) * HEAD_DIM]
                              for a in range(h * GROUP, (h + 1) * GROUP)], axis=0)
        hs = slice(h * HEAD_DIM, (h + 1) * HEAD_DIM)
        k_all = jnp.concatenate([k_ext[b * Q_BLOCK:(b + 3) * Q_BLOCK, hs], k_ctx[:, hs]], axis=0)
        return lax.dot_general(k_all, qs, contract_last, preferred_element_type=f32)

    s_ahead = scores(*units[0])
    for u, (b, h) in enumerate(units):
        s = s_ahead
        if u + 1 < len(units):
            s_ahead = scores(*units[u + 1])
        heads = [h * GROUP + g for g in range(GROUP)]
        hs = slice(h * HEAD_DIM, (h + 1) * HEAD_DIM)
        v_all = jnp.concatenate([v_ext[b * Q_BLOCK:(b + 3) * Q_BLOCK, hs], v_ctx[:, hs]], axis=0)
        bias_prev = edge_prev if b == 0 else tri_prev
        bias_next = edge_next if b == Q_BLOCKS_PER_STEP - 1 else tri_next
        s_prev = s[0:Q_BLOCK] + bias_prev
        s_cur = s[Q_BLOCK:2 * Q_BLOCK]
        s_next = s[2 * Q_BLOCK:3 * Q_BLOCK] + bias_next
        s_ctx = s[3 * Q_BLOCK:]
        sink = sink_ref[h] * LOG2E
        m = jnp.maximum(
            jnp.maximum(jnp.max(s_prev, axis=0, keepdims=True), jnp.max(s_cur, axis=0, keepdims=True)),
            jnp.maximum(jnp.max(s_next, axis=0, keepdims=True), jnp.max(s_ctx, axis=0, keepdims=True)))
        m = jnp.maximum(m, sink)
        e = jnp.exp2(jnp.concatenate([s_prev, s_cur, s_next, s_ctx], axis=0) - m)
        denom = jnp.sum(e, axis=0, keepdims=True) + jnp.exp2(sink - m)
        o_t = lax.dot_general(v_all, e.astype(bf16), contract_first, preferred_element_type=f32)
        o_t = o_t * (1.0 / denom)
        for pair in range(GROUP // 2):
            g = 2 * pair
            two = jnp.concatenate([o_t[:, g * Q_BLOCK:(g + 1) * Q_BLOCK],
                                   o_t[:, (g + 1) * Q_BLOCK:(g + 2) * Q_BLOCK]], axis=0)
            o_scr[b * Q_BLOCK:(b + 1) * Q_BLOCK, heads[g] * HEAD_DIM:(heads[g] + 2) * HEAD_DIM] = two.T

    y = jnp.dot(o_scr[...].astype(bf16), wo_ref[...], preferred_element_type=f32)
    gate = mod_ref[0, 5, pl.ds(pl.program_id(0), 1), :]
    o_ref[0] = x_ref[0] + gate * y


def _attn_mixer(x, ctx, mods, norm_g, w_qkv_b, w_o_b, sink, *, layer, member):
    batch, seq, _ = x.shape
    n_ctx = ctx.shape[1]
    assert seq % TOKEN_BLOCK == 0 and seq % Q_STEP == 0 and Q_BLOCK == WINDOW
    cos, sin = _rope_tables(seq)
    tm = TOKEN_BLOCK
    q, k, v = pl.pallas_call(
        _qkv_kernel,
        grid=(batch, seq // tm),
        in_specs=[
            _tok_spec(tm), _mod_spec(layer), _g_spec(layer), _resident(w_qkv_b, member),
            pl.BlockSpec((tm, LANES), lambda b, t: (t, 0)),
            pl.BlockSpec((tm, LANES), lambda b, t: (t, 0)),
        ],
        out_specs=[_tok_spec(tm, Q_DIM), _tok_spec(tm, KV_DIM), _tok_spec(tm, KV_DIM)],
        out_shape=[jax.ShapeDtypeStruct((batch, seq, Q_DIM), bf16),
                   jax.ShapeDtypeStruct((batch, seq, KV_DIM), bf16),
                   jax.ShapeDtypeStruct((batch, seq, KV_DIM), bf16)],
        compiler_params=_params(2),
        name=f"attn_qkv_l{layer}",
    )(x, mods, norm_g, w_qkv_b, cos, sin)

    k_ctx, v_ctx = pl.pallas_call(
        _kv_kernel,
        grid=(batch, 1),
        in_specs=[_tok_spec(n_ctx), _mod_spec(layer), _g_spec(layer), _resident(w_qkv_b, member)],
        out_specs=[_tok_spec(n_ctx, KV_DIM), _tok_spec(n_ctx, KV_DIM)],
        out_shape=[jax.ShapeDtypeStruct((batch, n_ctx, KV_DIM), bf16)] * 2,
        compiler_params=_params(2),
        name=f"attn_kv_ctx_l{layer}",
    )(ctx, mods, norm_g, w_qkv_b)

    n_blocks = seq // Q_BLOCK
    halo = lambda f: pl.BlockSpec((1, Q_BLOCK, KV_DIM), f)
    prev = halo(lambda b, n: (b, jnp.maximum(n * Q_BLOCKS_PER_STEP - 1, 0), 0))
    cur = pl.BlockSpec((1, Q_STEP, KV_DIM), lambda b, n: (b, n, 0))
    nxt = halo(lambda b, n: (b, jnp.minimum((n + 1) * Q_BLOCKS_PER_STEP, n_blocks - 1), 0))
    ctx_spec = pl.BlockSpec((1, n_ctx, KV_DIM), lambda b, n: (b, 0, 0))
    sink_cols = jnp.repeat(sink.reshape(N_KV_HEADS, GROUP), Q_BLOCK, axis=1).reshape(N_KV_HEADS, 1, GROUP * Q_BLOCK)
    return pl.pallas_call(
        _attn_kernel,
        grid=(batch, seq // Q_STEP),
        in_specs=[
            _tok_spec(Q_STEP, Q_DIM), prev, cur, nxt, prev, cur, nxt, ctx_spec, ctx_spec,
            pl.BlockSpec(sink_cols.shape, lambda b, n: (0, 0, 0)),
            _resident(w_o_b, member), _tok_spec(Q_STEP), _mod_spec(layer),
        ],
        out_specs=_tok_spec(Q_STEP),
        out_shape=jax.ShapeDtypeStruct(x.shape, f32),
        scratch_shapes=[pltpu.VMEM((Q_STEP, Q_DIM), f32)],
        compiler_params=_params(2),
        name=f"attn_core_l{layer}",
    )(q, k, k, k, v, v, v, k_ctx, v_ctx, sink_cols, w_o_b, x, mods)


def kernel(x, c, ctx, c_ctx, norm_g, ada_w, ada_b, ffn1_wi, ffn1_wo, ffn2_wi, ffn2_wo, conv_pw1_w, conv_pw1_b, conv_dw_w, conv_dw_b, conv_ln_g, conv_ln_b, conv_pw2_w, conv_pw2_b, attn_w_qkv, attn_w_o, attn_sink, final_g):
    mods = _ada_table(c, c_ctx, ada_w, ada_b)
    wi1, wo1, wi2, wo2 = (_to_bf16(w) for w in (ffn1_wi, ffn1_wo, ffn2_wi, ffn2_wo))
    pw1_b16, pw2_b16, qkv_b16, o_b16 = (_to_bf16(w) for w in (conv_pw1_w, conv_pw2_w, attn_w_qkv, attn_w_o))
    for i in range(DEPTH):
        last = i == DEPTH - 1
        j = i // 2
        ffn = functools.partial(_ffn, mods=mods, norm_g=norm_g, final_g=final_g, layer=i)

        x = ffn(x, wi=wi1, wo=wo1, sub=0, is_ctx=False)
        ctx = ffn(ctx, wi=wi1, wo=wo1, sub=0, is_ctx=True)

        if i % 2 == 0:
            conv = functools.partial(
                _conv_mixer, mods=mods, norm_g=norm_g, pw1_w=pw1_b16, pw1_b=conv_pw1_b[j],
                dw_w=conv_dw_w[j], dw_b=conv_dw_b[j], ln_g=conv_ln_g[j], ln_b=conv_ln_b[j],
                pw2_w=pw2_b16, pw2_b=conv_pw2_b[j], layer=i, member=j)
            x = conv(x, is_ctx=False)
            if not last:
                ctx = conv(ctx, is_ctx=True)
        else:
            assert last, "context outputs of the attention mixer are only needed by a following layer"
            x = _attn_mixer(x, ctx, mods, norm_g, qkv_b16, o_b16, attn_sink[j], layer=i, member=j)

        x = ffn(x, wi=wi2, wo=wo2, sub=2, is_ctx=False, final=last)
        if not last:
            ctx = ffn(ctx, wi=wi2, wo=wo2, sub=2, is_ctx=True)
    return x
```

```python
import functools

import jax
import jax.numpy as jnp
from jax import lax
from jax.experimental import pallas as pl
from jax.experimental.pallas import tpu as pltpu

D_MODEL = 1024
D_FF = 2816
DEPTH = 2
GRID_W = 64
CONV_WIDTH = 31
CONV_PAD = (CONV_WIDTH - 1) // 2
N_Q_HEADS = 16
N_KV_HEADS = 4
GROUP = N_Q_HEADS // N_KV_HEADS
HEAD_DIM = 64
WINDOW = 128
ROPE_BASE = 10000.0
NORM_EPS = 1e-6
N_MOD = 9
Q_DIM = N_Q_HEADS * HEAD_DIM
KV_DIM = N_KV_HEADS * HEAD_DIM

LANES = 128
SUBLANES = 8
MOD_ROWS = 8
CTX_ROW = 4
FFN_CHUNK = 256
N_FFN_CHUNKS = D_FF // FFN_CHUNK
TOKEN_BLOCK = 512
HALO = 16
CONV_PHASES = 4
VREG_ROWS = CONV_PHASES * SUBLANES
CONV_ROWS = 8 * VREG_ROWS
CONV_PARTS = 2
N_SLABS = D_MODEL // LANES
LOG2E = 1.4426950408889634
Q_BLOCK = 128
Q_BLOCKS_PER_STEP = 8
Q_STEP = Q_BLOCK * Q_BLOCKS_PER_STEP
FFN_TOKEN_BLOCK = 512
WEIGHT_SLOTS = 2
CAST_ROW_BLOCKS = 4
MASK_VALUE = -1e30
VMEM_LIMIT_BYTES = 56 * 1024 * 1024

f32 = jnp.float32
bf16 = jnp.bfloat16


def _sigmoid(x):
    return 1.0 / (1.0 + jnp.exp(-x))


def _silu(x):
    return x * _sigmoid(x)


def _params(n_grid):
    return pltpu.CompilerParams(
        dimension_semantics=("arbitrary",) * n_grid,
        vmem_limit_bytes=VMEM_LIMIT_BYTES)


def _resident(stack, index):
    return pl.BlockSpec((None,) + stack.shape[1:], lambda *_: (index, 0, 0), pipeline_mode=pl.Buffered(1))


def _cast_kernel(w_ref, o_ref):
    o_ref[...] = w_ref[...].astype(bf16)


def _to_bf16(w):
    n, rows, width = w.shape
    assert rows % (CAST_ROW_BLOCKS * 2 * SUBLANES) == 0
    spec = pl.BlockSpec((1, rows // CAST_ROW_BLOCKS, width), lambda l, r: (l, r, 0))
    return pl.pallas_call(
        _cast_kernel,
        grid=(n, CAST_ROW_BLOCKS),
        in_specs=[spec],
        out_specs=spec,
        out_shape=jax.ShapeDtypeStruct(w.shape, bf16),
        compiler_params=_params(2),
        name="cast_bf16",
    )(w)


def _modulated(x, mod_ref, g_ref, sub, row):
    shift = mod_ref[0, 3 * sub + 0, pl.ds(row, 1), :]
    scale = mod_ref[0, 3 * sub + 1, pl.ds(row, 1), :]
    gate = mod_ref[0, 3 * sub + 2, pl.ds(row, 1), :]
    g = g_ref[0, sub:sub + 1, :]
    y = x * lax.rsqrt(jnp.mean(x * x, axis=-1, keepdims=True) + NORM_EPS)
    h = (y * g) * (1.0 + scale) + shift
    return h.astype(bf16), gate


def _mod_row(is_ctx):
    return CTX_ROW if is_ctx else pl.program_id(0)


def _ada_kernel(c_ref, w_ref, b_ref, o_ref):
    s = _silu(c_ref[...]).astype(bf16)
    w = w_ref[0].astype(bf16)
    o_ref[0, 0] = jnp.dot(s, w, preferred_element_type=f32) + b_ref[0, 0]


def _ada_table(c, c_ctx, ada_w, ada_b):
    batch = c.shape[0]
    assert batch <= CTX_ROW < MOD_ROWS
    cvec = jnp.zeros((MOD_ROWS, D_MODEL), f32).at[:batch].set(c).at[CTX_ROW].set(c_ctx)
    bias = ada_b.reshape(DEPTH, N_MOD, 1, D_MODEL)
    return pl.pallas_call(
        _ada_kernel,
        grid=(DEPTH, N_MOD),
        in_specs=[
            pl.BlockSpec((MOD_ROWS, D_MODEL), lambda i, n: (0, 0)),
            pl.BlockSpec((1, D_MODEL, D_MODEL), lambda i, n: (i, 0, n)),
            pl.BlockSpec((1, 1, 1, D_MODEL), lambda i, n: (i, n, 0, 0)),
        ],
        out_specs=pl.BlockSpec((1, 1, MOD_ROWS, D_MODEL), lambda i, n: (i, n, 0, 0)),
        out_shape=jax.ShapeDtypeStruct((DEPTH, N_MOD, MOD_ROWS, D_MODEL), f32),
        compiler_params=_params(2),
        name="ada_table",
    )(cvec, ada_w, bias)


def _mod_spec(layer):
    return pl.BlockSpec((1, N_MOD, MOD_ROWS, D_MODEL), lambda *_: (layer, 0, 0, 0))


def _g_spec(layer):
    return pl.BlockSpec((1, 3, D_MODEL), lambda *_: (layer, 0, 0))


def _tok_spec(tm, width=D_MODEL):
    return pl.BlockSpec((1, tm, width), lambda b, t: (b, t, 0))


def _ffn_kernel(x_ref, mod_ref, g_ref, wi_hbm, wo_hbm, fg_ref, o_ref, wi_b, wo_b, stage_g, stage_u, stage_o, sems,
                *, layer, sub, blocks_per_batch, final):
    def chunk_copies(c, slot):
        lo = c * FFN_CHUNK
        return (
            pltpu.make_async_copy(wi_hbm.at[layer, :, pl.ds(lo, FFN_CHUNK)], stage_g.at[slot], sems.at[0, slot]),
            pltpu.make_async_copy(wi_hbm.at[layer, :, pl.ds(D_FF + lo, FFN_CHUNK)], stage_u.at[slot],
                                  sems.at[1, slot]),
            pltpu.make_async_copy(wo_hbm.at[layer, pl.ds(lo, FFN_CHUNK), :], stage_o.at[slot], sems.at[2, slot]),
        )

    def body(stream_weights):
        if stream_weights:
            for slot in range(WEIGHT_SLOTS):
                for copy in chunk_copies(slot, slot):
                    copy.start()
        x = x_ref[...]
        row = CTX_ROW if blocks_per_batch is None else pl.program_id(0) // blocks_per_batch
        h, gate = _modulated(x, mod_ref, g_ref, sub, row)
        acc = jnp.zeros(x.shape, f32)
        for c in range(N_FFN_CHUNKS):
            if stream_weights:
                slot = c % WEIGHT_SLOTS
                for copy in chunk_copies(c, slot):
                    copy.wait()
                wi_b[c, :, :FFN_CHUNK] = stage_g[slot].astype(bf16)
                wi_b[c, :, FFN_CHUNK:] = stage_u[slot].astype(bf16)
                wo_b[c] = stage_o[slot].astype(bf16)
                if c + WEIGHT_SLOTS < N_FFN_CHUNKS:
                    for copy in chunk_copies(c + WEIGHT_SLOTS, slot):
                        copy.start()
            gu = jnp.dot(h, wi_b[c], preferred_element_type=f32)
            a = _silu(gu[:, :FFN_CHUNK]) * gu[:, FFN_CHUNK:]
            acc = acc + jnp.dot(a.astype(bf16), wo_b[c], preferred_element_type=f32)
        out = x + 0.5 * gate * acc
        if final:
            out = out * lax.rsqrt(jnp.mean(out * out, axis=-1, keepdims=True) + NORM_EPS) * fg_ref[...]
        o_ref[...] = out

    first = pl.program_id(0) == 0
    pl.when(first)(functools.partial(body, True))
    pl.when(jnp.logical_not(first))(functools.partial(body, False))


def _ffn(x, mods, norm_g, wi, wo, final_g, *, layer, sub, is_ctx, final=False):
    batch, length, _ = x.shape
    tm = FFN_TOKEN_BLOCK
    xf = x.reshape(batch * length, D_MODEL)
    assert xf.shape[0] % tm == 0 and (is_ctx or length % tm == 0)
    spec = pl.BlockSpec((tm, D_MODEL), lambda i: (i, 0))
    hbm = pl.BlockSpec(memory_space=pl.ANY)
    out = pl.pallas_call(
        functools.partial(_ffn_kernel, layer=layer, sub=sub, blocks_per_batch=None if is_ctx else length // tm,
                          final=final),
        grid=(xf.shape[0] // tm,),
        in_specs=[spec, _mod_spec(layer), _g_spec(layer), hbm, hbm, pl.BlockSpec((1, D_MODEL), lambda i: (0, 0))],
        out_specs=spec,
        out_shape=jax.ShapeDtypeStruct(xf.shape, f32),
        scratch_shapes=[
            pltpu.VMEM((N_FFN_CHUNKS, D_MODEL, 2 * FFN_CHUNK), bf16),
            pltpu.VMEM((N_FFN_CHUNKS, FFN_CHUNK, D_MODEL), bf16),
            pltpu.VMEM((WEIGHT_SLOTS, D_MODEL, FFN_CHUNK), f32),
            pltpu.VMEM((WEIGHT_SLOTS, D_MODEL, FFN_CHUNK), f32),
            pltpu.VMEM((WEIGHT_SLOTS, FFN_CHUNK, D_MODEL), f32),
            pltpu.SemaphoreType.DMA((3, WEIGHT_SLOTS)),
        ],
        compiler_params=_params(1),
        name=f"ffn_l{layer}_s{sub}_{'ctx' if is_ctx else 'lat'}",
    )(xf, mods, norm_g, wi, wo, final_g.reshape(1, D_MODEL))
    return out.reshape(x.shape)


def _glu_kernel(x_ref, mod_ref, g_ref, w_ref, b_ref, u_ref, *, is_ctx):
    h, _ = _modulated(x_ref[0], mod_ref, g_ref, 1, _mod_row(is_ctx))
    a = jnp.dot(h, w_ref[...], preferred_element_type=f32) + b_ref[...]
    u_ref[0] = a[:, :D_MODEL] * _sigmoid(a[:, D_MODEL:])


def _conv_kernel(um_ref, up_ref, un_ref, x_ref, mod_ref, dww_ref, dwb_ref, lng_ref, lnb_ref,
                 w2_ref, b2_ref, o_ref, buf, conv, *, is_ctx):
    t = pl.program_id(1)
    tm = um_ref.shape[1]
    for j in range(N_SLABS):
        sl = slice(j * LANES, (j + 1) * LANES)
        buf[j, 0:HALO, :] = jnp.where(t > 0, up_ref[0, :, sl], 0.0)
        buf[j, HALO:HALO + tm, :] = um_ref[0, :, sl]
        buf[j, HALO + tm:, :] = jnp.where(t < pl.num_programs(1) - 1, un_ref[0, :, sl], 0.0)

    def step(i, carry):
        j = i % N_SLABS
        base = (i // N_SLABS) * CONV_ROWS
        outs = [(g, p) for g in range(CONV_ROWS // VREG_ROWS) for p in range(CONV_PHASES)]
        parts = {o: [None] * CONV_PARTS for o in outs}
        taps = {}
        for k in range(CONV_WIDTH):
            wk = dww_ref[j, k * SUBLANES:(k + 1) * SUBLANES, :]
            for g, p in outs:
                q = g * VREG_ROWS + p + k - CONV_PAD
                if q not in taps:
                    taps[q] = buf[j, pl.ds(base + HALO + q, SUBLANES, stride=CONV_PHASES), :]
                term = taps[q] * wk
                prev = parts[g, p][k % CONV_PARTS]
                parts[g, p][k % CONV_PARTS] = term if prev is None else prev + term
        for g, p in outs:
            total = parts[g, p][0]
            for part in parts[g, p][1:]:
                total = total + part
            conv[j, pl.ds(base + g * VREG_ROWS + p, SUBLANES, stride=CONV_PHASES), :] = total
        return carry

    lax.fori_loop(0, N_SLABS * (tm // CONV_ROWS), step, 0)

    u = jnp.concatenate([conv[j] for j in range(N_SLABS)], axis=1) + dwb_ref[...]
    mu = jnp.mean(u, axis=-1, keepdims=True)
    uc = u - mu
    var = jnp.mean(uc * uc, axis=-1, keepdims=True)
    un = (uc * lax.rsqrt(var + NORM_EPS)) * lng_ref[...] + lnb_ref[...]
    y = jnp.dot(_silu(un).astype(bf16), w2_ref[...], preferred_element_type=f32) + b2_ref[...]
    gate = mod_ref[0, 5, pl.ds(_mod_row(is_ctx), 1), :]
    o_ref[0] = x_ref[0] + gate * y


def _conv_mixer(x, mods, norm_g, pw1_w, pw1_b, dw_w, dw_b, ln_g, ln_b, pw2_w, pw2_b, *, layer, member, is_ctx):
    batch, length, _ = x.shape
    tm = min(TOKEN_BLOCK, length)
    assert length % tm == 0 and tm % HALO == 0 and tm % CONV_ROWS == 0
    tag = "ctx" if is_ctx else "lat"
    row = lambda v: v.reshape(1, -1)
    u = pl.pallas_call(
        functools.partial(_glu_kernel, is_ctx=is_ctx),
        grid=(batch, length // tm),
        in_specs=[
            _tok_spec(tm), _mod_spec(layer), _g_spec(layer),
            _resident(pw1_w, member), pl.BlockSpec((1, 2 * D_MODEL), lambda b, t: (0, 0)),
        ],
        out_specs=_tok_spec(tm),
        out_shape=jax.ShapeDtypeStruct(x.shape, f32),
        compiler_params=_params(2),
        name=f"conv_glu_l{layer}_{tag}",
    )(x, mods, norm_g, pw1_w, row(pw1_b))

    halo_per_block = tm // HALO
    n_halo = length // HALO
    vec = pl.BlockSpec((1, D_MODEL), lambda b, t: (0, 0))
    dww = jnp.repeat(dw_w.reshape(CONV_WIDTH, N_SLABS, LANES).transpose(1, 0, 2), SUBLANES, axis=1)
    return pl.pallas_call(
        functools.partial(_conv_kernel, is_ctx=is_ctx),
        grid=(batch, length // tm),
        in_specs=[
            _tok_spec(tm),
            pl.BlockSpec((1, HALO, D_MODEL), lambda b, t: (b, jnp.maximum(t * halo_per_block - 1, 0), 0)),
            pl.BlockSpec((1, HALO, D_MODEL),
                         lambda b, t: (b, jnp.minimum((t + 1) * halo_per_block, n_halo - 1), 0)),
            _tok_spec(tm), _mod_spec(layer),
            pl.BlockSpec(dww.shape, lambda b, t: (0, 0, 0)),
            vec, vec, vec, _resident(pw2_w, member), vec,
        ],
        out_specs=_tok_spec(tm),
        out_shape=jax.ShapeDtypeStruct(x.shape, f32),
        scratch_shapes=[pltpu.VMEM((N_SLABS, tm + 2 * HALO, LANES), f32), pltpu.VMEM((N_SLABS, tm, LANES), f32)],
        compiler_params=_params(2),
        name=f"conv_dw_l{layer}_{tag}",
    )(u, u, u, x, mods, dww, row(dw_b), row(ln_g), row(ln_b), pw2_w, row(pw2_b))


def _rope_tables(n_tok):
    n_freq = HEAD_DIM // 4
    pos = jnp.arange(n_tok)
    inv = ROPE_BASE ** (-jnp.arange(n_freq, dtype=f32) / n_freq)
    row = (pos // GRID_W).astype(f32)[:, None] * inv
    col = (pos % GRID_W).astype(f32)[:, None] * inv
    cos = jnp.concatenate([jnp.cos(row), jnp.cos(row), jnp.cos(col), jnp.cos(col)], axis=1)
    sin = jnp.concatenate([-jnp.sin(row), jnp.sin(row), -jnp.sin(col), jnp.sin(col)], axis=1)
    reps = LANES // HEAD_DIM
    return jnp.tile(cos, (1, reps)), jnp.tile(sin, (1, reps))


def _qkv_kernel(x_ref, mod_ref, g_ref, w_ref, cos_ref, sin_ref, q_ref, kv_ref):
    h, _ = _modulated(x_ref[0], mod_ref, g_ref, 1, _mod_row(False))
    qkv = jnp.dot(h, w_ref[...], preferred_element_type=f32)
    cos = cos_ref[...]
    sin = sin_ref[...]
    n_freq = HEAD_DIM // 4
    lane = lax.broadcasted_iota(jnp.int32, cos.shape, 1)
    is_x1 = (lane % (2 * n_freq)) < n_freq

    def rope(xb):
        partner = jnp.where(is_x1, pltpu.roll(xb, LANES - n_freq, 1), pltpu.roll(xb, n_freq, 1))
        return xb * cos + partner * sin

    for j in range(Q_DIM // LANES):
        sl = slice(j * LANES, (j + 1) * LANES)
        q_ref[0, :, sl] = (rope(qkv[:, sl]) * (HEAD_DIM ** -0.5 * LOG2E)).astype(bf16)
    for j in range(KV_DIM // LANES):
        sl = slice(j * LANES, (j + 1) * LANES)
        kv_ref[0, :, sl] = rope(qkv[:, Q_DIM + j * LANES:Q_DIM + (j + 1) * LANES]).astype(bf16)
    kv_ref[0, :, KV_DIM:] = qkv[:, Q_DIM + KV_DIM:].astype(bf16)


def _kv_kernel(x_ref, mod_ref, g_ref, w_ref, kv_ref):
    h, _ = _modulated(x_ref[0], mod_ref, g_ref, 1, _mod_row(True))
    kv_ref[0] = jnp.dot(h, w_ref[:, Q_DIM:], preferred_element_type=f32).astype(bf16)


def _attn_kernel(q_ref, kvp_ref, kvc_ref, kvn_ref, kvx_ref, sink_ref, wo_ref, x_ref, mod_ref, o_ref, o_scr):
    n = pl.program_id(1)
    n_steps = pl.num_programs(1)
    kv_ext = jnp.concatenate([kvp_ref[0], kvc_ref[0], kvn_ref[0]], axis=0)
    k_ext, v_ext = kv_ext[:, :KV_DIM], kv_ext[:, KV_DIM:]
    k_ctx, v_ctx = kvx_ref[0, :, :KV_DIM], kvx_ref[0, :, KV_DIM:]

    cols = GROUP * Q_BLOCK
    key = lax.broadcasted_iota(jnp.int32, (Q_BLOCK, cols), 0)
    qry = lax.broadcasted_iota(jnp.int32, (Q_BLOCK, cols), 1) % Q_BLOCK
    tri_prev = jnp.where(key >= qry, 0.0, MASK_VALUE)
    tri_next = jnp.where(key <= qry, 0.0, MASK_VALUE)
    edge_prev = jnp.where(n > 0, tri_prev, MASK_VALUE)
    edge_next = jnp.where(n < n_steps - 1, tri_next, MASK_VALUE)

    contract_last = (((1,), (1,)), ((), ()))
    contract_first = (((0,), (0,)), ((), ()))
    units = [(b, h) for b in range(Q_BLOCKS_PER_STEP) for h in range(N_KV_HEADS)]

    def scores(b, h):
        q = q_ref[0, b * Q_BLOCK:(b + 1) * Q_BLOCK, :]
        qs = jnp.concatenate([q[:, a * HEAD_DIM:(a + 1) * HEAD_DIM]
                              for a in range(h * GROUP, (h + 1) * GROUP)], axis=0)
        hs = slice(h * HEAD_DIM, (h + 1) * HEAD_DIM)
        k_all = jnp.concatenate([k_ext[b * Q_BLOCK:(b + 3) * Q_BLOCK, hs], k_ctx[:, hs]], axis=0)
        return lax.dot_general(k_all, qs, contract_last, preferred_element_type=f32)

    s_ahead = scores(*units[0])
    for u, (b, h) in enumerate(units):
        s = s_ahead
        if u + 1 < len(units):
            s_ahead = scores(*units[u + 1])
        heads = [h * GROUP + g for g in range(GROUP)]
        hs = slice(h * HEAD_DIM, (h + 1) * HEAD_DIM)
        v_all = jnp.concatenate([v_ext[b * Q_BLOCK:(b + 3) * Q_BLOCK, hs], v_ctx[:, hs]], axis=0)
        bias_prev = edge_prev if b == 0 else tri_prev
        bias_next = edge_next if b == Q_BLOCKS_PER_STEP - 1 else tri_next
        s_prev = s[0:Q_BLOCK] + bias_prev
        s_cur = s[Q_BLOCK:2 * Q_BLOCK]
        s_next = s[2 * Q_BLOCK:3 * Q_BLOCK] + bias_next
        s_ctx = s[3 * Q_BLOCK:]
        sink = sink_ref[h] * LOG2E
        m = jnp.maximum(
            jnp.maximum(jnp.max(s_prev, axis=0, keepdims=True), jnp.max(s_cur, axis=0, keepdims=True)),
            jnp.maximum(jnp.max(s_next, axis=0, keepdims=True), jnp.max(s_ctx, axis=0, keepdims=True)))
        m = jnp.maximum(m, sink)
        e = jnp.exp2(jnp.concatenate([s_prev, s_cur, s_next, s_ctx], axis=0) - m)
        denom = jnp.sum(e, axis=0, keepdims=True) + jnp.exp2(sink - m)
        o_t = lax.dot_general(v_all, e.astype(bf16), contract_first, preferred_element_type=f32)
        o_t = o_t * (1.0 / denom)
        for pair in range(GROUP // 2):
            g = 2 * pair
            two = jnp.concatenate([o_t[:, g * Q_BLOCK:(g + 1) * Q_BLOCK],
                                   o_t[:, (g + 1) * Q_BLOCK:(g + 2) * Q_BLOCK]], axis=0)
            o_scr[b * Q_BLOCK:(b + 1) * Q_BLOCK, heads[g] * HEAD_DIM:(heads[g] + 2) * HEAD_DIM] = two.T

    y = jnp.dot(o_scr[...].astype(bf16), wo_ref[...], preferred_element_type=f32)
    gate = mod_ref[0, 5, pl.ds(pl.program_id(0), 1), :]
    o_ref[0] = x_ref[0] + gate * y


def _attn_mixer(x, ctx, mods, norm_g, w_qkv_b, w_o_b, sink, *, layer, member):
    batch, seq, _ = x.shape
    n_ctx = ctx.shape[1]
    assert seq % TOKEN_BLOCK == 0 and seq % Q_STEP == 0 and Q_BLOCK == WINDOW
    cos, sin = _rope_tables(seq)
    tm = TOKEN_BLOCK
    q, kv = pl.pallas_call(
        _qkv_kernel,
        grid=(batch, seq // tm),
        in_specs=[
            _tok_spec(tm), _mod_spec(layer), _g_spec(layer), _resident(w_qkv_b, member),
            pl.BlockSpec((tm, LANES), lambda b, t: (t, 0)),
            pl.BlockSpec((tm, LANES), lambda b, t: (t, 0)),
        ],
        out_specs=[_tok_spec(tm, Q_DIM), _tok_spec(tm, 2 * KV_DIM)],
        out_shape=[jax.ShapeDtypeStruct((batch, seq, Q_DIM), bf16),
                   jax.ShapeDtypeStruct((batch, seq, 2 * KV_DIM), bf16)],
        compiler_params=_params(2),
        name=f"attn_qkv_l{layer}",
    )(x, mods, norm_g, w_qkv_b, cos, sin)

    kv_ctx = pl.pallas_call(
        _kv_kernel,
        grid=(batch, 1),
        in_specs=[_tok_spec(n_ctx), _mod_spec(layer), _g_spec(layer), _resident(w_qkv_b, member)],
        out_specs=_tok_spec(n_ctx, 2 * KV_DIM),
        out_shape=jax.ShapeDtypeStruct((batch, n_ctx, 2 * KV_DIM), bf16),
        compiler_params=_params(2),
        name=f"attn_kv_ctx_l{layer}",
    )(ctx, mods, norm_g, w_qkv_b)

    n_blocks = seq // Q_BLOCK
    halo = lambda f: pl.BlockSpec((1, Q_BLOCK, 2 * KV_DIM), f)
    prev = halo(lambda b, n: (b, jnp.maximum(n * Q_BLOCKS_PER_STEP - 1, 0), 0))
    cur = pl.BlockSpec((1, Q_STEP, 2 * KV_DIM), lambda b, n: (b, n, 0))
    nxt = halo(lambda b, n: (b, jnp.minimum((n + 1) * Q_BLOCKS_PER_STEP, n_blocks - 1), 0))
    ctx_spec = pl.BlockSpec((1, n_ctx, 2 * KV_DIM), lambda b, n: (b, 0, 0))
    sink_cols = jnp.repeat(sink.reshape(N_KV_HEADS, GROUP), Q_BLOCK, axis=1).reshape(N_KV_HEADS, 1, GROUP * Q_BLOCK)
    return pl.pallas_call(
        _attn_kernel,
        grid=(batch, seq // Q_STEP),
        in_specs=[
            _tok_spec(Q_STEP, Q_DIM), prev, cur, nxt, ctx_spec,
            pl.BlockSpec(sink_cols.shape, lambda b, n: (0, 0, 0)),
            _resident(w_o_b, member), _tok_spec(Q_STEP), _mod_spec(layer),
        ],
        out_specs=_tok_spec(Q_STEP),
        out_shape=jax.ShapeDtypeStruct(x.shape, f32),
        scratch_shapes=[pltpu.VMEM((Q_STEP, Q_DIM), f32)],
        compiler_params=_params(2),
        name=f"attn_core_l{layer}",
    )(q, kv, kv, kv, kv_ctx, sink_cols, w_o_b, x, mods)


def kernel(x, c, ctx, c_ctx, norm_g, ada_w, ada_b, ffn1_wi, ffn1_wo, ffn2_wi, ffn2_wo, conv_pw1_w, conv_pw1_b, conv_dw_w, conv_dw_b, conv_ln_g, conv_ln_b, conv_pw2_w, conv_pw2_b, attn_w_qkv, attn_w_o, attn_sink, final_g):
    mods = _ada_table(c, c_ctx, ada_w, ada_b)
    wi1, wo1, wi2, wo2 = ffn1_wi, ffn1_wo, ffn2_wi, ffn2_wo
    pw1_b16, pw2_b16, qkv_b16, o_b16 = (_to_bf16(w) for w in (conv_pw1_w, conv_pw2_w, attn_w_qkv, attn_w_o))
    for i in range(DEPTH):
        last = i == DEPTH - 1
        j = i // 2
        ffn = functools.partial(_ffn, mods=mods, norm_g=norm_g, final_g=final_g, layer=i)

        x = ffn(x, wi=wi1, wo=wo1, sub=0, is_ctx=False)
        ctx = ffn(ctx, wi=wi1, wo=wo1, sub=0, is_ctx=True)

        if i % 2 == 0:
            conv = functools.partial(
                _conv_mixer, mods=mods, norm_g=norm_g, pw1_w=pw1_b16, pw1_b=conv_pw1_b[j],
                dw_w=conv_dw_w[j], dw_b=conv_dw_b[j], ln_g=conv_ln_g[j], ln_b=conv_ln_b[j],
                pw2_w=pw2_b16, pw2_b=conv_pw2_b[j], layer=i, member=j)
            x = conv(x, is_ctx=False)
            if not last:
                ctx = conv(ctx, is_ctx=True)
        else:
            assert last, "context outputs of the attention mixer are only needed by a following layer"
            x = _attn_mixer(x, ctx, mods, norm_g, qkv_b16, o_b16, attn_sink[j], layer=i, member=j)

        x = ffn(x, wi=wi2, wo=wo2, sub=2, is_ctx=False, final=last)
        if not last:
            ctx = ffn(ctx, wi=wi2, wo=wo2, sub=2, is_ctx=True)
    return x
```

```python
import functools

import jax
import jax.numpy as jnp
from jax import lax
from jax.experimental import pallas as pl
from jax.experimental.pallas import tpu as pltpu

D_MODEL = 1024
D_FF = 2816
DEPTH = 2
GRID_W = 64
CONV_WIDTH = 31
CONV_PAD = (CONV_WIDTH - 1) // 2
N_Q_HEADS = 16
N_KV_HEADS = 4
GROUP = N_Q_HEADS // N_KV_HEADS
HEAD_DIM = 64
WINDOW = 128
ROPE_BASE = 10000.0
NORM_EPS = 1e-6
N_MOD = 9
Q_DIM = N_Q_HEADS * HEAD_DIM
KV_DIM = N_KV_HEADS * HEAD_DIM

LANES = 128
SUBLANES = 8
MOD_ROWS = 8
CTX_ROW = 4
FFN_CHUNK = 256
N_FFN_CHUNKS = D_FF // FFN_CHUNK
TOKEN_BLOCK = 512
HALO = 16
CONV_PHASES = 4
VREG_ROWS = CONV_PHASES * SUBLANES
CONV_ROWS = 4 * VREG_ROWS
N_SLABS = D_MODEL // LANES
LOG2E = 1.4426950408889634
Q_BLOCK = 128
Q_BLOCKS_PER_STEP = 8
Q_STEP = Q_BLOCK * Q_BLOCKS_PER_STEP
FFN_TOKEN_BLOCK = 512
CTX_FFN_TOKEN_BLOCK = 1024
WEIGHT_SLOTS = 2
CAST_ROW_BLOCKS = 4
MASK_VALUE = -1e30
VMEM_LIMIT_BYTES = 56 * 1024 * 1024

f32 = jnp.float32
bf16 = jnp.bfloat16


def _sigmoid(x):
    return 1.0 / (1.0 + jnp.exp(-x))


def _silu(x):
    return x * _sigmoid(x)


def _params(n_grid):
    return pltpu.CompilerParams(
        dimension_semantics=("arbitrary",) * n_grid,
        vmem_limit_bytes=VMEM_LIMIT_BYTES)


def _resident(stack, index):
    return pl.BlockSpec((None,) + stack.shape[1:], lambda *_: (index, 0, 0), pipeline_mode=pl.Buffered(1))


def _cast_kernel(w_ref, o_ref):
    o_ref[...] = w_ref[...].astype(bf16)


def _to_bf16(w):
    n, rows, width = w.shape
    assert rows % (CAST_ROW_BLOCKS * 2 * SUBLANES) == 0
    spec = pl.BlockSpec((1, rows // CAST_ROW_BLOCKS, width), lambda l, r: (l, r, 0))
    return pl.pallas_call(
        _cast_kernel,
        grid=(n, CAST_ROW_BLOCKS),
        in_specs=[spec],
        out_specs=spec,
        out_shape=jax.ShapeDtypeStruct(w.shape, bf16),
        compiler_params=_params(2),
        name="cast_bf16",
    )(w)


def _modulated(x, mod_ref, g_ref, sub, row):
    shift = mod_ref[0, 3 * sub + 0, pl.ds(row, 1), :]
    scale = mod_ref[0, 3 * sub + 1, pl.ds(row, 1), :]
    gate = mod_ref[0, 3 * sub + 2, pl.ds(row, 1), :]
    g = g_ref[0, sub:sub + 1, :]
    y = x * lax.rsqrt(jnp.mean(x * x, axis=-1, keepdims=True) + NORM_EPS)
    h = (y * g) * (1.0 + scale) + shift
    return h.astype(bf16), gate


def _mod_row(is_ctx):
    return CTX_ROW if is_ctx else pl.program_id(0)


def _ada_kernel(c_ref, w_ref, b_ref, o_ref):
    s = _silu(c_ref[...]).astype(bf16)
    w = w_ref[0].astype(bf16)
    o_ref[0, 0] = jnp.dot(s, w, preferred_element_type=f32) + b_ref[0, 0]


def _ada_table(c, c_ctx, ada_w, ada_b):
    batch = c.shape[0]
    assert batch <= CTX_ROW < MOD_ROWS
    cvec = jnp.zeros((MOD_ROWS, D_MODEL), f32).at[:batch].set(c).at[CTX_ROW].set(c_ctx)
    bias = ada_b.reshape(DEPTH, N_MOD, 1, D_MODEL)
    return pl.pallas_call(
        _ada_kernel,
        grid=(DEPTH, N_MOD),
        in_specs=[
            pl.BlockSpec((MOD_ROWS, D_MODEL), lambda i, n: (0, 0)),
            pl.BlockSpec((1, D_MODEL, D_MODEL), lambda i, n: (i, 0, n)),
            pl.BlockSpec((1, 1, 1, D_MODEL), lambda i, n: (i, n, 0, 0)),
        ],
        out_specs=pl.BlockSpec((1, 1, MOD_ROWS, D_MODEL), lambda i, n: (i, n, 0, 0)),
        out_shape=jax.ShapeDtypeStruct((DEPTH, N_MOD, MOD_ROWS, D_MODEL), f32),
        compiler_params=_params(2),
        name="ada_table",
    )(cvec, ada_w, bias)


def _mod_spec(layer):
    return pl.BlockSpec((1, N_MOD, MOD_ROWS, D_MODEL), lambda *_: (layer, 0, 0, 0))


def _g_spec(layer):
    return pl.BlockSpec((1, 3, D_MODEL), lambda *_: (layer, 0, 0))


def _tok_spec(tm, width=D_MODEL):
    return pl.BlockSpec((1, tm, width), lambda b, t: (b, t, 0))


def _ffn_kernel(x_ref, mod_ref, g_ref, wi_hbm, wo_hbm, fg_ref, o_ref, wi_b, wo_b, stage_g, stage_u, stage_o, sems,
                *, layer, sub, blocks_per_batch, final):
    def chunk_copies(c, slot):
        lo = c * FFN_CHUNK
        return (
            pltpu.make_async_copy(wi_hbm.at[layer, :, pl.ds(lo, FFN_CHUNK)], stage_g.at[slot], sems.at[0, slot]),
            pltpu.make_async_copy(wi_hbm.at[layer, :, pl.ds(D_FF + lo, FFN_CHUNK)], stage_u.at[slot],
                                  sems.at[1, slot]),
            pltpu.make_async_copy(wo_hbm.at[layer, pl.ds(lo, FFN_CHUNK), :], stage_o.at[slot], sems.at[2, slot]),
        )

    def body(stream_weights):
        if stream_weights:
            for slot in range(WEIGHT_SLOTS):
                for copy in chunk_copies(slot, slot):
                    copy.start()
        x = x_ref[...]
        row = CTX_ROW if blocks_per_batch is None else pl.program_id(0) // blocks_per_batch
        h, gate = _modulated(x, mod_ref, g_ref, sub, row)
        acc = jnp.zeros(x.shape, f32)
        for c in range(N_FFN_CHUNKS):
            if stream_weights:
                slot = c % WEIGHT_SLOTS
                for copy in chunk_copies(c, slot):
                    copy.wait()
                wi_b[c, :, :FFN_CHUNK] = stage_g[slot].astype(bf16)
                wi_b[c, :, FFN_CHUNK:] = stage_u[slot].astype(bf16)
                wo_b[c] = stage_o[slot].astype(bf16)
                if c + WEIGHT_SLOTS < N_FFN_CHUNKS:
                    for copy in chunk_copies(c + WEIGHT_SLOTS, slot):
                        copy.start()
            gu = jnp.dot(h, wi_b[c], preferred_element_type=f32)
            a = _silu(gu[:, :FFN_CHUNK]) * gu[:, FFN_CHUNK:]
            acc = acc + jnp.dot(a.astype(bf16), wo_b[c], preferred_element_type=f32)
        out = x + 0.5 * gate * acc
        if final:
            out = out * lax.rsqrt(jnp.mean(out * out, axis=-1, keepdims=True) + NORM_EPS) * fg_ref[...]
        o_ref[...] = out

    first = pl.program_id(0) == 0
    pl.when(first)(functools.partial(body, True))
    pl.when(jnp.logical_not(first))(functools.partial(body, False))


def _ffn(x, mods, norm_g, wi, wo, final_g, *, layer, sub, is_ctx, final=False):
    batch, length, _ = x.shape
    tm = min(batch * length, CTX_FFN_TOKEN_BLOCK) if is_ctx else FFN_TOKEN_BLOCK
    xf = x.reshape(batch * length, D_MODEL)
    assert xf.shape[0] % tm == 0 and (is_ctx or length % tm == 0)
    spec = pl.BlockSpec((tm, D_MODEL), lambda i: (i, 0))
    hbm = pl.BlockSpec(memory_space=pl.ANY)
    out = pl.pallas_call(
        functools.partial(_ffn_kernel, layer=layer, sub=sub, blocks_per_batch=None if is_ctx else length // tm,
                          final=final),
        grid=(xf.shape[0] // tm,),
        in_specs=[spec, _mod_spec(layer), _g_spec(layer), hbm, hbm, pl.BlockSpec((1, D_MODEL), lambda i: (0, 0))],
        out_specs=spec,
        out_shape=jax.ShapeDtypeStruct(xf.shape, f32),
        scratch_shapes=[
            pltpu.VMEM((N_FFN_CHUNKS, D_MODEL, 2 * FFN_CHUNK), bf16),
            pltpu.VMEM((N_FFN_CHUNKS, FFN_CHUNK, D_MODEL), bf16),
            pltpu.VMEM((WEIGHT_SLOTS, D_MODEL, FFN_CHUNK), f32),
            pltpu.VMEM((WEIGHT_SLOTS, D_MODEL, FFN_CHUNK), f32),
            pltpu.VMEM((WEIGHT_SLOTS, FFN_CHUNK, D_MODEL), f32),
            pltpu.SemaphoreType.DMA((3, WEIGHT_SLOTS)),
        ],
        compiler_params=_params(1),
        name=f"ffn_l{layer}_s{sub}_{'ctx' if is_ctx else 'lat'}",
    )(xf, mods, norm_g, wi, wo, final_g.reshape(1, D_MODEL))
    return out.reshape(x.shape)


def _glu_kernel(x_ref, mod_ref, g_ref, w_ref, b_ref, u_ref, *, is_ctx):
    h, _ = _modulated(x_ref[0], mod_ref, g_ref, 1, _mod_row(is_ctx))
    a = jnp.dot(h, w_ref[...], preferred_element_type=f32) + b_ref[...]
    u_ref[0] = a[:, :D_MODEL] * _sigmoid(a[:, D_MODEL:])


def _conv_kernel(um_ref, up_ref, un_ref, x_ref, mod_ref, dww_ref, dwb_ref, lng_ref, lnb_ref,
                 w2_ref, b2_ref, o_ref, buf, conv, *, is_ctx):
    t = pl.program_id(1)
    tm = um_ref.shape[1]
    for j in range(N_SLABS):
        sl = slice(j * LANES, (j + 1) * LANES)
        buf[j, 0:HALO, :] = jnp.where(t > 0, up_ref[0, :, sl], 0.0)
        buf[j, HALO:HALO + tm, :] = um_ref[0, :, sl]
        buf[j, HALO + tm:, :] = jnp.where(t < pl.num_programs(1) - 1, un_ref[0, :, sl], 0.0)

    def step(i, carry):
        j = i % N_SLABS
        base = (i // N_SLABS) * CONV_ROWS
        for pair in range(CONV_ROWS // (2 * VREG_ROWS)):
            lo = base + pair * 2 * VREG_ROWS
            accs = [None] * CONV_PHASES
            taps = {}
            for k in range(CONV_WIDTH):
                wk = dww_ref[j, k * 2 * SUBLANES:(k + 1) * 2 * SUBLANES, :].astype(f32)
                for p in range(CONV_PHASES):
                    q = p + k - CONV_PAD
                    if q not in taps:
                        rows = [buf[j, pl.ds(lo + g * VREG_ROWS + HALO + q, SUBLANES, stride=CONV_PHASES), :]
                                for g in range(2)]
                        taps[q] = jnp.concatenate(rows, axis=0).astype(bf16)
                    term = taps[q].astype(f32) * wk
                    accs[p] = term if accs[p] is None else accs[p] + term
            for p in range(CONV_PHASES):
                for g in range(2):
                    conv[j, pl.ds(lo + g * VREG_ROWS + p, SUBLANES, stride=CONV_PHASES), :] = (
                        accs[p][g * SUBLANES:(g + 1) * SUBLANES])
        return carry

    lax.fori_loop(0, N_SLABS * (tm // CONV_ROWS), step, 0)

    u = jnp.concatenate([conv[j] for j in range(N_SLABS)], axis=1) + dwb_ref[...]
    mu = jnp.mean(u, axis=-1, keepdims=True)
    uc = u - mu
    var = jnp.mean(uc * uc, axis=-1, keepdims=True)
    un = (uc * lax.rsqrt(var + NORM_EPS)) * lng_ref[...] + lnb_ref[...]
    y = jnp.dot(_silu(un).astype(bf16), w2_ref[...], preferred_element_type=f32) + b2_ref[...]
    gate = mod_ref[0, 5, pl.ds(_mod_row(is_ctx), 1), :]
    o_ref[0] = x_ref[0] + gate * y


def _conv_mixer(x, mods, norm_g, pw1_w, pw1_b, dw_w, dw_b, ln_g, ln_b, pw2_w, pw2_b, *, layer, member, is_ctx):
    batch, length, _ = x.shape
    tm = min(TOKEN_BLOCK, length)
    assert length % tm == 0 and tm % HALO == 0 and tm % CONV_ROWS == 0
    tag = "ctx" if is_ctx else "lat"
    row = lambda v: v.reshape(1, -1)
    u = pl.pallas_call(
        functools.partial(_glu_kernel, is_ctx=is_ctx),
        grid=(batch, length // tm),
        in_specs=[
            _tok_spec(tm), _mod_spec(layer), _g_spec(layer),
            _resident(pw1_w, member), pl.BlockSpec((1, 2 * D_MODEL), lambda b, t: (0, 0)),
        ],
        out_specs=_tok_spec(tm),
        out_shape=jax.ShapeDtypeStruct(x.shape, f32),
        compiler_params=_params(2),
        name=f"conv_glu_l{layer}_{tag}",
    )(x, mods, norm_g, pw1_w, row(pw1_b))

    halo_per_block = tm // HALO
    n_halo = length // HALO
    vec = pl.BlockSpec((1, D_MODEL), lambda b, t: (0, 0))
    dww = jnp.repeat(dw_w.reshape(CONV_WIDTH, N_SLABS, LANES).transpose(1, 0, 2), 2 * SUBLANES, axis=1).astype(bf16)
    return pl.pallas_call(
        functools.partial(_conv_kernel, is_ctx=is_ctx),
        grid=(batch, length // tm),
        in_specs=[
            _tok_spec(tm),
            pl.BlockSpec((1, HALO, D_MODEL), lambda b, t: (b, jnp.maximum(t * halo_per_block - 1, 0), 0)),
            pl.BlockSpec((1, HALO, D_MODEL),
                         lambda b, t: (b, jnp.minimum((t + 1) * halo_per_block, n_halo - 1), 0)),
            _tok_spec(tm), _mod_spec(layer),
            pl.BlockSpec(dww.shape, lambda b, t: (0, 0, 0)),
            vec, vec, vec, _resident(pw2_w, member), vec,
        ],
        out_specs=_tok_spec(tm),
        out_shape=jax.ShapeDtypeStruct(x.shape, f32),
        scratch_shapes=[pltpu.VMEM((N_SLABS, tm + 2 * HALO, LANES), f32), pltpu.VMEM((N_SLABS, tm, LANES), f32)],
        compiler_params=_params(2),
        name=f"conv_dw_l{layer}_{tag}",
    )(u, u, u, x, mods, dww, row(dw_b), row(ln_g), row(ln_b), pw2_w, row(pw2_b))


def _rope_tables(n_tok):
    n_freq = HEAD_DIM // 4
    pos = jnp.arange(n_tok)
    inv = ROPE_BASE ** (-jnp.arange(n_freq, dtype=f32) / n_freq)
    row = (pos // GRID_W).astype(f32)[:, None] * inv
    col = (pos % GRID_W).astype(f32)[:, None] * inv
    cos = jnp.concatenate([jnp.cos(row), jnp.cos(row), jnp.cos(col), jnp.cos(col)], axis=1)
    sin = jnp.concatenate([-jnp.sin(row), jnp.sin(row), -jnp.sin(col), jnp.sin(col)], axis=1)
    reps = LANES // HEAD_DIM
    return jnp.tile(cos, (1, reps)), jnp.tile(sin, (1, reps))


def _qkv_kernel(x_ref, mod_ref, g_ref, w_ref, cos_ref, sin_ref, q_ref, kv_ref):
    h, _ = _modulated(x_ref[0], mod_ref, g_ref, 1, _mod_row(False))
    qkv = jnp.dot(h, w_ref[...], preferred_element_type=f32)
    cos = cos_ref[...]
    sin = sin_ref[...]
    n_freq = HEAD_DIM // 4
    lane = lax.broadcasted_iota(jnp.int32, cos.shape, 1)
    is_x1 = (lane % (2 * n_freq)) < n_freq

    def rope(xb):
        partner = jnp.where(is_x1, pltpu.roll(xb, LANES - n_freq, 1), pltpu.roll(xb, n_freq, 1))
        return xb * cos + partner * sin

    for j in range(Q_DIM // LANES):
        sl = slice(j * LANES, (j + 1) * LANES)
        q_ref[0, :, sl] = (rope(qkv[:, sl]) * (HEAD_DIM ** -0.5 * LOG2E)).astype(bf16)
    for j in range(KV_DIM // LANES):
        sl = slice(j * LANES, (j + 1) * LANES)
        kv_ref[0, :, sl] = rope(qkv[:, Q_DIM + j * LANES:Q_DIM + (j + 1) * LANES]).astype(bf16)
    kv_ref[0, :, KV_DIM:] = qkv[:, Q_DIM + KV_DIM:].astype(bf16)


def _kv_kernel(x_ref, mod_ref, g_ref, w_ref, kv_ref):
    h, _ = _modulated(x_ref[0], mod_ref, g_ref, 1, _mod_row(True))
    kv_ref[0] = jnp.dot(h, w_ref[:, Q_DIM:], preferred_element_type=f32).astype(bf16)


def _attn_kernel(q_ref, kvp_ref, kvc_ref, kvn_ref, kvx_ref, sink_ref, wo_ref, x_ref, mod_ref, o_ref, o_scr):
    n = pl.program_id(1)
    n_steps = pl.num_programs(1)
    kv_ext = jnp.concatenate([kvp_ref[0], kvc_ref[0], kvn_ref[0]], axis=0)
    k_ext, v_ext = kv_ext[:, :KV_DIM], kv_ext[:, KV_DIM:]
    k_ctx, v_ctx = kvx_ref[0, :, :KV_DIM], kvx_ref[0, :, KV_DIM:]

    cols = GROUP * Q_BLOCK
    key = lax.broadcasted_iota(jnp.int32, (Q_BLOCK, cols), 0)
    qry = lax.broadcasted_iota(jnp.int32, (Q_BLOCK, cols), 1) % Q_BLOCK
    tri_prev = jnp.where(key >= qry, 0.0, MASK_VALUE)
    tri_next = jnp.where(key <= qry, 0.0, MASK_VALUE)
    edge_prev = jnp.where(n > 0, tri_prev, MASK_VALUE)
    edge_next = jnp.where(n < n_steps - 1, tri_next, MASK_VALUE)

    contract_last = (((1,), (1,)), ((), ()))
    contract_first = (((0,), (0,)), ((), ()))
    units = [(b, h) for b in range(Q_BLOCKS_PER_STEP) for h in range(N_KV_HEADS)]

    def scores(b, h):
        q = q_ref[0, b * Q_BLOCK:(b + 1) * Q_BLOCK, :]
        qs = jnp.concatenate([q[:, a * HEAD_DIM:(a + 1) * HEAD_DIM]
                              for a in range(h * GROUP, (h + 1) * GROUP)], axis=0)
        hs = slice(h * HEAD_DIM, (h + 1) * HEAD_DIM)
        k_all = jnp.concatenate([k_ext[b * Q_BLOCK:(b + 3) * Q_BLOCK, hs], k_ctx[:, hs]], axis=0)
        return lax.dot_general(k_all, qs, contract_last, preferred_element_type=f32)

    s_ahead = scores(*units[0])
    for u, (b, h) in enumerate(units):
        s = s_ahead
        if u + 1 < len(units):
            s_ahead = scores(*units[u + 1])
        heads = [h * GROUP + g for g in range(GROUP)]
        hs = slice(h * HEAD_DIM, (h + 1) * HEAD_DIM)
        v_all = jnp.concatenate([v_ext[b * Q_BLOCK:(b + 3) * Q_BLOCK, hs], v_ctx[:, hs]], axis=0)
        bias_prev = edge_prev if b == 0 else tri_prev
        bias_next = edge_next if b == Q_BLOCKS_PER_STEP - 1 else tri_next
        s_prev = s[0:Q_BLOCK] + bias_prev
        s_cur = s[Q_BLOCK:2 * Q_BLOCK]
        s_next = s[2 * Q_BLOCK:3 * Q_BLOCK] + bias_next
        s_ctx = s[3 * Q_BLOCK:]
        sink = sink_ref[h] * LOG2E
        m = jnp.maximum(
            jnp.maximum(jnp.max(s_prev, axis=0, keepdims=True), jnp.max(s_cur, axis=0, keepdims=True)),
            jnp.maximum(jnp.max(s_next, axis=0, keepdims=True), jnp.max(s_ctx, axis=0, keepdims=True)))
        m = jnp.maximum(m, sink)
        e = jnp.exp2(jnp.concatenate([s_prev, s_cur, s_next, s_ctx], axis=0) - m)
        denom = jnp.sum(e, axis=0, keepdims=True) + jnp.exp2(sink - m)
        o_t = lax.dot_general(v_all, e.astype(bf16), contract_first, preferred_element_type=f32)
        o_t = o_t * (1.0 / denom)
        for pair in range(GROUP // 2):
            g = 2 * pair
            two = jnp.concatenate([o_t[:, g * Q_BLOCK:(g + 1) * Q_BLOCK],
                                   o_t[:, (g + 1) * Q_BLOCK:(g + 2) * Q_BLOCK]], axis=0)
            o_scr[b * Q_BLOCK:(b + 1) * Q_BLOCK, heads[g] * HEAD_DIM:(heads[g] + 2) * HEAD_DIM] = two.T

    y = jnp.dot(o_scr[...].astype(bf16), wo_ref[...], preferred_element_type=f32)
    gate = mod_ref[0, 5, pl.ds(pl.program_id(0), 1), :]
    o_ref[0] = x_ref[0] + gate * y


def _attn_mixer(x, ctx, mods, norm_g, w_qkv_b, w_o_b, sink, *, layer, member):
    batch, seq, _ = x.shape
    n_ctx = ctx.shape[1]
    assert seq % TOKEN_BLOCK == 0 and seq % Q_STEP == 0 and Q_BLOCK == WINDOW
    cos, sin = _rope_tables(seq)
    tm = TOKEN_BLOCK
    q, kv = pl.pallas_call(
        _qkv_kernel,
        grid=(batch, seq // tm),
        in_specs=[
            _tok_spec(tm), _mod_spec(layer), _g_spec(layer), _resident(w_qkv_b, member),
            pl.BlockSpec((tm, LANES), lambda b, t: (t, 0)),
            pl.BlockSpec((tm, LANES), lambda b, t: (t, 0)),
        ],
        out_specs=[_tok_spec(tm, Q_DIM), _tok_spec(tm, 2 * KV_DIM)],
        out_shape=[jax.ShapeDtypeStruct((batch, seq, Q_DIM), bf16),
                   jax.ShapeDtypeStruct((batch, seq, 2 * KV_DIM), bf16)],
        compiler_params=_params(2),
        name=f"attn_qkv_l{layer}",
    )(x, mods, norm_g, w_qkv_b, cos, sin)

    kv_ctx = pl.pallas_call(
        _kv_kernel,
        grid=(batch, 1),
        in_specs=[_tok_spec(n_ctx), _mod_spec(layer), _g_spec(layer), _resident(w_qkv_b, member)],
        out_specs=_tok_spec(n_ctx, 2 * KV_DIM),
        out_shape=jax.ShapeDtypeStruct((batch, n_ctx, 2 * KV_DIM), bf16),
        compiler_params=_params(2),
        name=f"attn_kv_ctx_l{layer}",
    )(ctx, mods, norm_g, w_qkv_b)

    n_blocks = seq // Q_BLOCK
    halo = lambda f: pl.BlockSpec((1, Q_BLOCK, 2 * KV_DIM), f)
    prev = halo(lambda b, n: (b, jnp.maximum(n * Q_BLOCKS_PER_STEP - 1, 0), 0))
    cur = pl.BlockSpec((1, Q_STEP, 2 * KV_DIM), lambda b, n: (b, n, 0))
    nxt = halo(lambda b, n: (b, jnp.minimum((n + 1) * Q_BLOCKS_PER_STEP, n_blocks - 1), 0))
    ctx_spec = pl.BlockSpec((1, n_ctx, 2 * KV_DIM), lambda b, n: (b, 0, 0))
    sink_cols = jnp.repeat(sink.reshape(N_KV_HEADS, GROUP), Q_BLOCK, axis=1).reshape(N_KV_HEADS, 1, GROUP * Q_BLOCK)
    return pl.pallas_call(
        _attn_kernel,
        grid=(batch, seq // Q_STEP),
        in_specs=[
            _tok_spec(Q_STEP, Q_DIM), prev, cur, nxt, ctx_spec,
            pl.BlockSpec(sink_cols.shape, lambda b, n: (0, 0, 0)),
            _resident(w_o_b, member), _tok_spec(Q_STEP), _mod_spec(layer),
        ],
        out_specs=_tok_spec(Q_STEP),
        out_shape=jax.ShapeDtypeStruct(x.shape, f32),
        scratch_shapes=[pltpu.VMEM((Q_STEP, Q_DIM), f32)],
        compiler_params=_params(2),
        name=f"attn_core_l{layer}",
    )(q, kv, kv, kv, kv_ctx, sink_cols, w_o_b, x, mods)


def kernel(x, c, ctx, c_ctx, norm_g, ada_w, ada_b, ffn1_wi, ffn1_wo, ffn2_wi, ffn2_wo, conv_pw1_w, conv_pw1_b, conv_dw_w, conv_dw_b, conv_ln_g, conv_ln_b, conv_pw2_w, conv_pw2_b, attn_w_qkv, attn_w_o, attn_sink, final_g):
    mods = _ada_table(c, c_ctx, ada_w, ada_b)
    wi1, wo1, wi2, wo2 = ffn1_wi, ffn1_wo, ffn2_wi, ffn2_wo
    pw1_b16, pw2_b16, qkv_b16, o_b16 = (_to_bf16(w) for w in (conv_pw1_w, conv_pw2_w, attn_w_qkv, attn_w_o))
    for i in range(DEPTH):
        last = i == DEPTH - 1
        j = i // 2
        ffn = functools.partial(_ffn, mods=mods, norm_g=norm_g, final_g=final_g, layer=i)

        x = ffn(x, wi=wi1, wo=wo1, sub=0, is_ctx=False)
        ctx = ffn(ctx, wi=wi1, wo=wo1, sub=0, is_ctx=True)

        if i % 2 == 0:
            conv = functools.partial(
                _conv_mixer, mods=mods, norm_g=norm_g, pw1_w=pw1_b16, pw1_b=conv_pw1_b[j],
                dw_w=conv_dw_w[j], dw_b=conv_dw_b[j], ln_g=conv_ln_g[j], ln_b=conv_ln_b[j],
                pw2_w=pw2_b16, pw2_b=conv_pw2_b[j], layer=i, member=j)
            x = conv(x, is_ctx=False)
            if not last:
                ctx = conv(ctx, is_ctx=True)
        else:
            assert last, "context outputs of the attention mixer are only needed by a following layer"
            x = _attn_mixer(x, ctx, mods, norm_g, qkv_b16, o_b16, attn_sink[j], layer=i, member=j)

        x = ffn(x, wi=wi2, wo=wo2, sub=2, is_ctx=False, final=last)
        if not last:
            ctx = ffn(ctx, wi=wi2, wo=wo2, sub=2, is_ctx=True)
    return x
```

```python
import functools

import jax
import jax.numpy as jnp
from jax import lax
from jax.experimental import pallas as pl
from jax.experimental.pallas import tpu as pltpu

D_MODEL = 1024
D_FF = 2816
DEPTH = 2
GRID_W = 64
CONV_WIDTH = 31
CONV_PAD = (CONV_WIDTH - 1) // 2
N_Q_HEADS = 16
N_KV_HEADS = 4
GROUP = N_Q_HEADS // N_KV_HEADS
HEAD_DIM = 64
WINDOW = 128
ROPE_BASE = 10000.0
NORM_EPS = 1e-6
N_MOD = 9
Q_DIM = N_Q_HEADS * HEAD_DIM
KV_DIM = N_KV_HEADS * HEAD_DIM

LANES = 128
SUBLANES = 8
MOD_ROWS = 8
CTX_ROW = 4
FFN_CHUNK = 256
N_FFN_CHUNKS = D_FF // FFN_CHUNK
TOKEN_BLOCK = 512
HALO = 16
CONV_PHASES = 4
VREG_ROWS = CONV_PHASES * SUBLANES
CONV_ROWS = 4 * VREG_ROWS
N_SLABS = D_MODEL // LANES
LOG2E = 1.4426950408889634
Q_BLOCK = 128
Q_BLOCKS_PER_STEP = 8
Q_STEP = Q_BLOCK * Q_BLOCKS_PER_STEP
FFN_TOKEN_BLOCK = 512
WEIGHT_SLOTS = 2
CAST_ROW_BLOCKS = 4
MASK_VALUE = -1e30
VMEM_LIMIT_BYTES = 56 * 1024 * 1024

f32 = jnp.float32
bf16 = jnp.bfloat16


def _sigmoid(x):
    return 1.0 / (1.0 + jnp.exp(-x))


def _silu(x):
    return x * _sigmoid(x)


def _params(n_grid):
    return pltpu.CompilerParams(
        dimension_semantics=("arbitrary",) * n_grid,
        vmem_limit_bytes=VMEM_LIMIT_BYTES)


def _resident(stack, index):
    return pl.BlockSpec((None,) + stack.shape[1:], lambda *_: (index, 0, 0), pipeline_mode=pl.Buffered(1))


def _cast_kernel(w_ref, o_ref):
    o_ref[...] = w_ref[...].astype(bf16)


def _to_bf16(w):
    n, rows, width = w.shape
    assert rows % (CAST_ROW_BLOCKS * 2 * SUBLANES) == 0
    spec = pl.BlockSpec((1, rows // CAST_ROW_BLOCKS, width), lambda l, r: (l, r, 0))
    return pl.pallas_call(
        _cast_kernel,
        grid=(n, CAST_ROW_BLOCKS),
        in_specs=[spec],
        out_specs=spec,
        out_shape=jax.ShapeDtypeStruct(w.shape, bf16),
        compiler_params=_params(2),
        name="cast_bf16",
    )(w)


def _modulated(x, mod_ref, g_ref, sub, row):
    shift = mod_ref[0, 3 * sub + 0, pl.ds(row, 1), :]
    scale = mod_ref[0, 3 * sub + 1, pl.ds(row, 1), :]
    gate = mod_ref[0, 3 * sub + 2, pl.ds(row, 1), :]
    g = g_ref[0, sub:sub + 1, :]
    y = x * lax.rsqrt(jnp.mean(x * x, axis=-1, keepdims=True) + NORM_EPS)
    h = (y * g) * (1.0 + scale) + shift
    return h.astype(bf16), gate


def _mod_row(is_ctx):
    return CTX_ROW if is_ctx else pl.program_id(0)


def _ada_kernel(c_ref, w_ref, b_ref, o_ref):
    s = _silu(c_ref[...]).astype(bf16)
    w = w_ref[0].astype(bf16)
    o_ref[0, 0] = jnp.dot(s, w, preferred_element_type=f32) + b_ref[0, 0]


def _ada_table(c, c_ctx, ada_w, ada_b):
    batch = c.shape[0]
    assert batch <= CTX_ROW < MOD_ROWS
    cvec = jnp.zeros((MOD_ROWS, D_MODEL), f32).at[:batch].set(c).at[CTX_ROW].set(c_ctx)
    bias = ada_b.reshape(DEPTH, N_MOD, 1, D_MODEL)
    return pl.pallas_call(
        _ada_kernel,
        grid=(DEPTH, N_MOD),
        in_specs=[
            pl.BlockSpec((MOD_ROWS, D_MODEL), lambda i, n: (0, 0)),
            pl.BlockSpec((1, D_MODEL, D_MODEL), lambda i, n: (i, 0, n)),
            pl.BlockSpec((1, 1, 1, D_MODEL), lambda i, n: (i, n, 0, 0)),
        ],
        out_specs=pl.BlockSpec((1, 1, MOD_ROWS, D_MODEL), lambda i, n: (i, n, 0, 0)),
        out_shape=jax.ShapeDtypeStruct((DEPTH, N_MOD, MOD_ROWS, D_MODEL), f32),
        compiler_params=_params(2),
        name="ada_table",
    )(cvec, ada_w, bias)


def _mod_spec(layer):
    return pl.BlockSpec((1, N_MOD, MOD_ROWS, D_MODEL), lambda *_: (layer, 0, 0, 0))


def _g_spec(layer):
    return pl.BlockSpec((1, 3, D_MODEL), lambda *_: (layer, 0, 0))


def _tok_spec(tm, width=D_MODEL):
    return pl.BlockSpec((1, tm, width), lambda b, t: (b, t, 0))


def _ffn_kernel(*refs, layer, sub, n_lat, blocks_per_batch, has_ctx, final):
    if has_ctx:
        (x_ref, c_ref, mod_ref, g_ref, wi_hbm, wo_hbm, fg_ref, ox_ref, oc_ref,
         wi_b, wo_b, stage_g, stage_u, stage_o, sems) = refs
    else:
        x_ref, mod_ref, g_ref, wi_hbm, wo_hbm, fg_ref, ox_ref, wi_b, wo_b, stage_g, stage_u, stage_o, sems = refs

    def chunk_copies(c, slot):
        lo = c * FFN_CHUNK
        return (
            pltpu.make_async_copy(wi_hbm.at[layer, :, pl.ds(lo, FFN_CHUNK)], stage_g.at[slot], sems.at[0, slot]),
            pltpu.make_async_copy(wi_hbm.at[layer, :, pl.ds(D_FF + lo, FFN_CHUNK)], stage_u.at[slot],
                                  sems.at[1, slot]),
            pltpu.make_async_copy(wo_hbm.at[layer, pl.ds(lo, FFN_CHUNK), :], stage_o.at[slot], sems.at[2, slot]),
        )

    def body(src_ref, dst_ref, row, stream_weights):
        if stream_weights:
            for slot in range(WEIGHT_SLOTS):
                for copy in chunk_copies(slot, slot):
                    copy.start()
        x = src_ref[...]
        h, gate = _modulated(x, mod_ref, g_ref, sub, row)
        acc = jnp.zeros(x.shape, f32)
        for c in range(N_FFN_CHUNKS):
            if stream_weights:
                slot = c % WEIGHT_SLOTS
                for copy in chunk_copies(c, slot):
                    copy.wait()
                wi_b[c, :, :FFN_CHUNK] = stage_g[slot].astype(bf16)
                wi_b[c, :, FFN_CHUNK:] = stage_u[slot].astype(bf16)
                wo_b[c] = stage_o[slot].astype(bf16)
                if c + WEIGHT_SLOTS < N_FFN_CHUNKS:
                    for copy in chunk_copies(c + WEIGHT_SLOTS, slot):
                        copy.start()
            gu = jnp.dot(h, wi_b[c], preferred_element_type=f32)
            a = _silu(gu[:, :FFN_CHUNK]) * gu[:, FFN_CHUNK:]
            acc = acc + jnp.dot(a.astype(bf16), wo_b[c], preferred_element_type=f32)
        out = x + 0.5 * gate * acc
        if final:
            out = out * lax.rsqrt(jnp.mean(out * out, axis=-1, keepdims=True) + NORM_EPS) * fg_ref[...]
        dst_ref[...] = out

    i = pl.program_id(0)
    lat_row = i // blocks_per_batch
    pl.when(i == 0)(functools.partial(body, x_ref, ox_ref, lat_row, True))
    pl.when(jnp.logical_and(i > 0, i < n_lat))(functools.partial(body, x_ref, ox_ref, lat_row, False))
    if has_ctx:
        pl.when(i >= n_lat)(functools.partial(body, c_ref, oc_ref, CTX_ROW, False))


def _ffn(x, ctx, mods, norm_g, wi, wo, final_g, *, layer, sub, final=False):
    batch, seq, _ = x.shape
    tm = FFN_TOKEN_BLOCK
    assert seq % tm == 0
    has_ctx = ctx is not None
    n_lat = batch * seq // tm
    xf = x.reshape(batch * seq, D_MODEL)
    x_spec = pl.BlockSpec((tm, D_MODEL), lambda i: (jnp.minimum(i, n_lat - 1), 0))
    hbm = pl.BlockSpec(memory_space=pl.ANY)
    shared = [_mod_spec(layer), _g_spec(layer), hbm, hbm, pl.BlockSpec((1, D_MODEL), lambda i: (0, 0))]
    consts = (mods, norm_g, wi, wo, final_g.reshape(1, D_MODEL))
    if has_ctx:
        cf = ctx.reshape(-1, D_MODEL)
        assert cf.shape[0] % tm == 0
        n_ctx = cf.shape[0] // tm
        c_spec = pl.BlockSpec((tm, D_MODEL), lambda i: (jnp.maximum(i - n_lat, 0), 0))
        in_specs, args = [x_spec, c_spec] + shared, (xf, cf) + consts
        out_specs = [x_spec, c_spec]
        out_shape = [jax.ShapeDtypeStruct(xf.shape, f32), jax.ShapeDtypeStruct(cf.shape, f32)]
    else:
        n_ctx = 0
        in_specs, args = [x_spec] + shared, (xf,) + consts
        out_specs = x_spec
        out_shape = jax.ShapeDtypeStruct(xf.shape, f32)
    out = pl.pallas_call(
        functools.partial(_ffn_kernel, layer=layer, sub=sub, n_lat=n_lat, blocks_per_batch=seq // tm,
                          has_ctx=has_ctx, final=final),
        grid=(n_lat + n_ctx,),
        in_specs=in_specs,
        out_specs=out_specs,
        out_shape=out_shape,
        scratch_shapes=[
            pltpu.VMEM((N_FFN_CHUNKS, D_MODEL, 2 * FFN_CHUNK), bf16),
            pltpu.VMEM((N_FFN_CHUNKS, FFN_CHUNK, D_MODEL), bf16),
            pltpu.VMEM((WEIGHT_SLOTS, D_MODEL, FFN_CHUNK), f32),
            pltpu.VMEM((WEIGHT_SLOTS, D_MODEL, FFN_CHUNK), f32),
            pltpu.VMEM((WEIGHT_SLOTS, FFN_CHUNK, D_MODEL), f32),
            pltpu.SemaphoreType.DMA((3, WEIGHT_SLOTS)),
        ],
        compiler_params=_params(1),
        name=f"ffn_l{layer}_s{sub}",
    )(*args)
    if has_ctx:
        return out[0].reshape(x.shape), out[1].reshape(ctx.shape)
    return out.reshape(x.shape), None


def _glu_kernel(x_ref, mod_ref, g_ref, w_ref, b_ref, u_ref, *, is_ctx):
    h, _ = _modulated(x_ref[0], mod_ref, g_ref, 1, _mod_row(is_ctx))
    a = jnp.dot(h, w_ref[...], preferred_element_type=f32) + b_ref[...]
    u_ref[0] = a[:, :D_MODEL] * _sigmoid(a[:, D_MODEL:])


def _conv_kernel(um_ref, up_ref, un_ref, x_ref, mod_ref, dww_ref, dwb_ref, lng_ref, lnb_ref,
                 w2_ref, b2_ref, o_ref, buf, conv, *, is_ctx):
    t = pl.program_id(1)
    tm = um_ref.shape[1]
    for j in range(N_SLABS):
        sl = slice(j * LANES, (j + 1) * LANES)
        buf[j, 0:HALO, :] = jnp.where(t > 0, up_ref[0, :, sl], 0.0)
        buf[j, HALO:HALO + tm, :] = um_ref[0, :, sl]
        buf[j, HALO + tm:, :] = jnp.where(t < pl.num_programs(1) - 1, un_ref[0, :, sl], 0.0)

    def step(i, carry):
        j = i % N_SLABS
        base = (i // N_SLABS) * CONV_ROWS
        for pair in range(CONV_ROWS // (2 * VREG_ROWS)):
            lo = base + pair * 2 * VREG_ROWS
            accs = [None] * CONV_PHASES
            taps = {}
            for k in range(CONV_WIDTH):
                wk = dww_ref[j, k * 2 * SUBLANES:(k + 1) * 2 * SUBLANES, :].astype(f32)
                for p in range(CONV_PHASES):
                    q = p + k - CONV_PAD
                    if q not in taps:
                        rows = [buf[j, pl.ds(lo + g * VREG_ROWS + HALO + q, SUBLANES, stride=CONV_PHASES), :]
                                for g in range(2)]
                        taps[q] = jnp.concatenate(rows, axis=0).astype(bf16)
                    term = taps[q].astype(f32) * wk
                    accs[p] = term if accs[p] is None else accs[p] + term
            for p in range(CONV_PHASES):
                for g in range(2):
                    conv[j, pl.ds(lo + g * VREG_ROWS + p, SUBLANES, stride=CONV_PHASES), :] = (
                        accs[p][g * SUBLANES:(g + 1) * SUBLANES])
        return carry

    lax.fori_loop(0, N_SLABS * (tm // CONV_ROWS), step, 0)

    u = jnp.concatenate([conv[j] for j in range(N_SLABS)], axis=1) + dwb_ref[...]
    mu = jnp.mean(u, axis=-1, keepdims=True)
    uc = u - mu
    var = jnp.mean(uc * uc, axis=-1, keepdims=True)
    un = (uc * lax.rsqrt(var + NORM_EPS)) * lng_ref[...] + lnb_ref[...]
    y = jnp.dot(_silu(un).astype(bf16), w2_ref[...], preferred_element_type=f32) + b2_ref[...]
    gate = mod_ref[0, 5, pl.ds(_mod_row(is_ctx), 1), :]
    o_ref[0] = x_ref[0] + gate * y


def _conv_mixer(x, mods, norm_g, pw1_w, pw1_b, dw_w, dw_b, ln_g, ln_b, pw2_w, pw2_b, *, layer, member, is_ctx):
    batch, length, _ = x.shape
    tm = min(TOKEN_BLOCK, length)
    assert length % tm == 0 and tm % HALO == 0 and tm % CONV_ROWS == 0
    tag = "ctx" if is_ctx else "lat"
    row = lambda v: v.reshape(1, -1)
    u = pl.pallas_call(
        functools.partial(_glu_kernel, is_ctx=is_ctx),
        grid=(batch, length // tm),
        in_specs=[
            _tok_spec(tm), _mod_spec(layer), _g_spec(layer),
            _resident(pw1_w, member), pl.BlockSpec((1, 2 * D_MODEL), lambda b, t: (0, 0)),
        ],
        out_specs=_tok_spec(tm),
        out_shape=jax.ShapeDtypeStruct(x.shape, f32),
        compiler_params=_params(2),
        name=f"conv_glu_l{layer}_{tag}",
    )(x, mods, norm_g, pw1_w, row(pw1_b))

    halo_per_block = tm // HALO
    n_halo = length // HALO
    vec = pl.BlockSpec((1, D_MODEL), lambda b, t: (0, 0))
    dww = jnp.repeat(dw_w.reshape(CONV_WIDTH, N_SLABS, LANES).transpose(1, 0, 2), 2 * SUBLANES, axis=1).astype(bf16)
    return pl.pallas_call(
        functools.partial(_conv_kernel, is_ctx=is_ctx),
        grid=(batch, length // tm),
        in_specs=[
            _tok_spec(tm),
            pl.BlockSpec((1, HALO, D_MODEL), lambda b, t: (b, jnp.maximum(t * halo_per_block - 1, 0), 0)),
            pl.BlockSpec((1, HALO, D_MODEL),
                         lambda b, t: (b, jnp.minimum((t + 1) * halo_per_block, n_halo - 1), 0)),
            _tok_spec(tm), _mod_spec(layer),
            pl.BlockSpec(dww.shape, lambda b, t: (0, 0, 0)),
            vec, vec, vec, _resident(pw2_w, member), vec,
        ],
        out_specs=_tok_spec(tm),
        out_shape=jax.ShapeDtypeStruct(x.shape, f32),
        scratch_shapes=[pltpu.VMEM((N_SLABS, tm + 2 * HALO, LANES), f32), pltpu.VMEM((N_SLABS, tm, LANES), f32)],
        compiler_params=_params(2),
        name=f"conv_dw_l{layer}_{tag}",
    )(u, u, u, x, mods, dww, row(dw_b), row(ln_g), row(ln_b), pw2_w, row(pw2_b))


def _rope_tables(n_tok):
    n_freq = HEAD_DIM // 4
    pos = jnp.arange(n_tok)
    inv = ROPE_BASE ** (-jnp.arange(n_freq, dtype=f32) / n_freq)
    row = (pos // GRID_W).astype(f32)[:, None] * inv
    col = (pos % GRID_W).astype(f32)[:, None] * inv
    cos = jnp.concatenate([jnp.cos(row), jnp.cos(row), jnp.cos(col), jnp.cos(col)], axis=1)
    sin = jnp.concatenate([-jnp.sin(row), jnp.sin(row), -jnp.sin(col), jnp.sin(col)], axis=1)
    reps = LANES // HEAD_DIM
    return jnp.tile(cos, (1, reps)), jnp.tile(sin, (1, reps))


def _qkv_kernel(x_ref, mod_ref, g_ref, w_ref, cos_ref, sin_ref, q_ref, kv_ref):
    h, _ = _modulated(x_ref[0], mod_ref, g_ref, 1, _mod_row(False))
    qkv = jnp.dot(h, w_ref[...], preferred_element_type=f32)
    cos = cos_ref[...]
    sin = sin_ref[...]
    n_freq = HEAD_DIM // 4
    lane = lax.broadcasted_iota(jnp.int32, cos.shape, 1)
    is_x1 = (lane % (2 * n_freq)) < n_freq

    def rope(xb):
        partner = jnp.where(is_x1, pltpu.roll(xb, LANES - n_freq, 1), pltpu.roll(xb, n_freq, 1))
        return xb * cos + partner * sin

    for j in range(Q_DIM // LANES):
        sl = slice(j * LANES, (j + 1) * LANES)
        q_ref[0, :, sl] = (rope(qkv[:, sl]) * (HEAD_DIM ** -0.5 * LOG2E)).astype(bf16)
    for j in range(KV_DIM // LANES):
        sl = slice(j * LANES, (j + 1) * LANES)
        kv_ref[0, :, sl] = rope(qkv[:, Q_DIM + j * LANES:Q_DIM + (j + 1) * LANES]).astype(bf16)
    kv_ref[0, :, KV_DIM:] = qkv[:, Q_DIM + KV_DIM:].astype(bf16)


def _kv_kernel(x_ref, mod_ref, g_ref, w_ref, kv_ref):
    h, _ = _modulated(x_ref[0], mod_ref, g_ref, 1, _mod_row(True))
    kv_ref[0] = jnp.dot(h, w_ref[:, Q_DIM:], preferred_element_type=f32).astype(bf16)


def _attn_kernel(q_ref, kvp_ref, kvc_ref, kvn_ref, kvx_ref, sink_ref, wo_ref, x_ref, mod_ref, o_ref, o_scr):
    n = pl.program_id(1)
    n_steps = pl.num_programs(1)
    kv_ext = jnp.concatenate([kvp_ref[0], kvc_ref[0], kvn_ref[0]], axis=0)
    k_ext, v_ext = kv_ext[:, :KV_DIM], kv_ext[:, KV_DIM:]
    k_ctx, v_ctx = kvx_ref[0, :, :KV_DIM], kvx_ref[0, :, KV_DIM:]

    cols = GROUP * Q_BLOCK
    key = lax.broadcasted_iota(jnp.int32, (Q_BLOCK, cols), 0)
    qry = lax.broadcasted_iota(jnp.int32, (Q_BLOCK, cols), 1) % Q_BLOCK
    tri_prev = jnp.where(key >= qry, 0.0, MASK_VALUE)
    tri_next = jnp.where(key <= qry, 0.0, MASK_VALUE)
    edge_prev = jnp.where(n > 0, tri_prev, MASK_VALUE)
    edge_next = jnp.where(n < n_steps - 1, tri_next, MASK_VALUE)

    contract_last = (((1,), (1,)), ((), ()))
    contract_first = (((0,), (0,)), ((), ()))
    units = [(b, h) for b in range(Q_BLOCKS_PER_STEP) for h in range(N_KV_HEADS)]

    def scores(b, h):
        q = q_ref[0, b * Q_BLOCK:(b + 1) * Q_BLOCK, :]
        qs = jnp.concatenate([q[:, a * HEAD_DIM:(a + 1) * HEAD_DIM]
                              for a in range(h * GROUP, (h + 1) * GROUP)], axis=0)
        hs = slice(h * HEAD_DIM, (h + 1) * HEAD_DIM)
        k_all = jnp.concatenate([k_ext[b * Q_BLOCK:(b + 3) * Q_BLOCK, hs], k_ctx[:, hs]], axis=0)
        return lax.dot_general(k_all, qs, contract_last, preferred_element_type=f32)

    s_ahead = scores(*units[0])
    for u, (b, h) in enumerate(units):
        s = s_ahead
        if u + 1 < len(units):
            s_ahead = scores(*units[u + 1])
        heads = [h * GROUP + g for g in range(GROUP)]
        hs = slice(h * HEAD_DIM, (h + 1) * HEAD_DIM)
        v_all = jnp.concatenate([v_ext[b * Q_BLOCK:(b + 3) * Q_BLOCK, hs], v_ctx[:, hs]], axis=0)
        bias_prev = edge_prev if b == 0 else tri_prev
        bias_next = edge_next if b == Q_BLOCKS_PER_STEP - 1 else tri_next
        s_prev = s[0:Q_BLOCK] + bias_prev
        s_cur = s[Q_BLOCK:2 * Q_BLOCK]
        s_next = s[2 * Q_BLOCK:3 * Q_BLOCK] + bias_next
        s_ctx = s[3 * Q_BLOCK:]
        sink = sink_ref[h] * LOG2E
        m = jnp.maximum(
            jnp.maximum(jnp.max(s_prev, axis=0, keepdims=True), jnp.max(s_cur, axis=0, keepdims=True)),
            jnp.maximum(jnp.max(s_next, axis=0, keepdims=True), jnp.max(s_ctx, axis=0, keepdims=True)))
        m = jnp.maximum(m, sink)
        e = jnp.exp2(jnp.concatenate([s_prev, s_cur, s_next, s_ctx], axis=0) - m)
        denom = jnp.sum(e, axis=0, keepdims=True) + jnp.exp2(sink - m)
        o_t = lax.dot_general(v_all, e.astype(bf16), contract_first, preferred_element_type=f32)
        o_t = o_t * (1.0 / denom)
        for pair in range(GROUP // 2):
            g = 2 * pair
            two = jnp.concatenate([o_t[:, g * Q_BLOCK:(g + 1) * Q_BLOCK],
                                   o_t[:, (g + 1) * Q_BLOCK:(g + 2) * Q_BLOCK]], axis=0)
            o_scr[b * Q_BLOCK:(b + 1) * Q_BLOCK, heads[g] * HEAD_DIM:(heads[g] + 2) * HEAD_DIM] = two.T

    y = jnp.dot(o_scr[...].astype(bf16), wo_ref[...], preferred_element_type=f32)
    gate = mod_ref[0, 5, pl.ds(pl.program_id(0), 1), :]
    o_ref[0] = x_ref[0] + gate * y


def _attn_mixer(x, ctx, mods, norm_g, w_qkv_b, w_o_b, sink, *, layer, member):
    batch, seq, _ = x.shape
    n_ctx = ctx.shape[1]
    assert seq % TOKEN_BLOCK == 0 and seq % Q_STEP == 0 and Q_BLOCK == WINDOW
    cos, sin = _rope_tables(seq)
    tm = TOKEN_BLOCK
    q, kv = pl.pallas_call(
        _qkv_kernel,
        grid=(batch, seq // tm),
        in_specs=[
            _tok_spec(tm), _mod_spec(layer), _g_spec(layer), _resident(w_qkv_b, member),
            pl.BlockSpec((tm, LANES), lambda b, t: (t, 0)),
            pl.BlockSpec((tm, LANES), lambda b, t: (t, 0)),
        ],
        out_specs=[_tok_spec(tm, Q_DIM), _tok_spec(tm, 2 * KV_DIM)],
        out_shape=[jax.ShapeDtypeStruct((batch, seq, Q_DIM), bf16),
                   jax.ShapeDtypeStruct((batch, seq, 2 * KV_DIM), bf16)],
        compiler_params=_params(2),
        name=f"attn_qkv_l{layer}",
    )(x, mods, norm_g, w_qkv_b, cos, sin)

    kv_ctx = pl.pallas_call(
        _kv_kernel,
        grid=(batch, 1),
        in_specs=[_tok_spec(n_ctx), _mod_spec(layer), _g_spec(layer), _resident(w_qkv_b, member)],
        out_specs=_tok_spec(n_ctx, 2 * KV_DIM),
        out_shape=jax.ShapeDtypeStruct((batch, n_ctx, 2 * KV_DIM), bf16),
        compiler_params=_params(2),
        name=f"attn_kv_ctx_l{layer}",
    )(ctx, mods, norm_g, w_qkv_b)

    n_blocks = seq // Q_BLOCK
    halo = lambda f: pl.BlockSpec((1, Q_BLOCK, 2 * KV_DIM), f)
    prev = halo(lambda b, n: (b, jnp.maximum(n * Q_BLOCKS_PER_STEP - 1, 0), 0))
    cur = pl.BlockSpec((1, Q_STEP, 2 * KV_DIM), lambda b, n: (b, n, 0))
    nxt = halo(lambda b, n: (b, jnp.minimum((n + 1) * Q_BLOCKS_PER_STEP, n_blocks - 1), 0))
    ctx_spec = pl.BlockSpec((1, n_ctx, 2 * KV_DIM), lambda b, n: (b, 0, 0))
    sink_cols = jnp.repeat(sink.reshape(N_KV_HEADS, GROUP), Q_BLOCK, axis=1).reshape(N_KV_HEADS, 1, GROUP * Q_BLOCK)
    return pl.pallas_call(
        _attn_kernel,
        grid=(batch, seq // Q_STEP),
        in_specs=[
            _tok_spec(Q_STEP, Q_DIM), prev, cur, nxt, ctx_spec,
            pl.BlockSpec(sink_cols.shape, lambda b, n: (0, 0, 0)),
            _resident(w_o_b, member), _tok_spec(Q_STEP), _mod_spec(layer),
        ],
        out_specs=_tok_spec(Q_STEP),
        out_shape=jax.ShapeDtypeStruct(x.shape, f32),
        scratch_shapes=[pltpu.VMEM((Q_STEP, Q_DIM), f32)],
        compiler_params=_params(2),
        name=f"attn_core_l{layer}",
    )(q, kv, kv, kv, kv_ctx, sink_cols, w_o_b, x, mods)


def kernel(x, c, ctx, c_ctx, norm_g, ada_w, ada_b, ffn1_wi, ffn1_wo, ffn2_wi, ffn2_wo, conv_pw1_w, conv_pw1_b, conv_dw_w, conv_dw_b, conv_ln_g, conv_ln_b, conv_pw2_w, conv_pw2_b, attn_w_qkv, attn_w_o, attn_sink, final_g):
    mods = _ada_table(c, c_ctx, ada_w, ada_b)
    wi1, wo1, wi2, wo2 = ffn1_wi, ffn1_wo, ffn2_wi, ffn2_wo
    pw1_b16, pw2_b16, qkv_b16, o_b16 = (_to_bf16(w) for w in (conv_pw1_w, conv_pw2_w, attn_w_qkv, attn_w_o))
    for i in range(DEPTH):
        last = i == DEPTH - 1
        j = i // 2
        ffn = functools.partial(_ffn, mods=mods, norm_g=norm_g, final_g=final_g, layer=i)

        x, ctx = ffn(x, ctx, wi=wi1, wo=wo1, sub=0)

        if i % 2 == 0:
            conv = functools.partial(
                _conv_mixer, mods=mods, norm_g=norm_g, pw1_w=pw1_b16, pw1_b=conv_pw1_b[j],
                dw_w=conv_dw_w[j], dw_b=conv_dw_b[j], ln_g=conv_ln_g[j], ln_b=conv_ln_b[j],
                pw2_w=pw2_b16, pw2_b=conv_pw2_b[j], layer=i, member=j)
            x = conv(x, is_ctx=False)
            if not last:
                ctx = conv(ctx, is_ctx=True)
        else:
            assert last, "context outputs of the attention mixer are only needed by a following layer"
            x = _attn_mixer(x, ctx, mods, norm_g, qkv_b16, o_b16, attn_sink[j], layer=i, member=j)

        x, ctx = ffn(x, None if last else ctx, wi=wi2, wo=wo2, sub=2, final=last)
    return x
```

```python
import functools

import jax
import jax.numpy as jnp
from jax import lax
from jax.experimental import pallas as pl
from jax.experimental.pallas import tpu as pltpu

D_MODEL = 1024
D_FF = 2816
DEPTH = 2
GRID_W = 64
CONV_WIDTH = 31
CONV_PAD = (CONV_WIDTH - 1) // 2
N_Q_HEADS = 16
N_KV_HEADS = 4
GROUP = N_Q_HEADS // N_KV_HEADS
HEAD_DIM = 64
WINDOW = 128
ROPE_BASE = 10000.0
NORM_EPS = 1e-6
N_MOD = 9
Q_DIM = N_Q_HEADS * HEAD_DIM
KV_DIM = N_KV_HEADS * HEAD_DIM

LANES = 128
SUBLANES = 8
MOD_ROWS = 8
CTX_ROW = 4
ADA_GROUP = 3
FFN_CHUNK = 256
N_FFN_CHUNKS = D_FF // FFN_CHUNK
TOKEN_BLOCK = 512
PROJ_TOKEN_BLOCK = 1024
HALO = 16
CONV_PHASES = 4
VREG_ROWS = CONV_PHASES * SUBLANES
CONV_ROWS = 4 * VREG_ROWS
N_SLABS = D_MODEL // LANES
LOG2E = 1.4426950408889634
Q_BLOCK = 128
Q_BLOCKS_PER_STEP = 8
Q_STEP = Q_BLOCK * Q_BLOCKS_PER_STEP
FFN_TOKEN_BLOCK = 512
CTX_FFN_TOKEN_BLOCK = 1024
WEIGHT_SLOTS = 2
CAST_ROW_BLOCKS = 4
MASK_VALUE = -1e30
VMEM_LIMIT_BYTES = 56 * 1024 * 1024

f32 = jnp.float32
bf16 = jnp.bfloat16


def _sigmoid(x):
    return 1.0 / (1.0 + jnp.exp(-x))


def _silu(x):
    return x * _sigmoid(x)


def _params(n_grid):
    return pltpu.CompilerParams(
        dimension_semantics=("arbitrary",) * n_grid,
        vmem_limit_bytes=VMEM_LIMIT_BYTES)


def _resident(stack, index):
    return pl.BlockSpec((None,) + stack.shape[1:], lambda *_: (index, 0, 0), pipeline_mode=pl.Buffered(1))


def _cast_kernel(w_ref, o_ref):
    o_ref[...] = w_ref[...].astype(bf16)


def _to_bf16(w):
    n, rows, width = w.shape
    assert rows % (CAST_ROW_BLOCKS * 2 * SUBLANES) == 0
    spec = pl.BlockSpec((1, rows // CAST_ROW_BLOCKS, width), lambda l, r: (l, r, 0))
    return pl.pallas_call(
        _cast_kernel,
        grid=(n, CAST_ROW_BLOCKS),
        in_specs=[spec],
        out_specs=spec,
        out_shape=jax.ShapeDtypeStruct(w.shape, bf16),
        compiler_params=_params(2),
        name="cast_bf16",
    )(w)


def _modulated(x, mod_ref, g_ref, sub, row):
    shift = mod_ref[0, 3 * sub + 0, pl.ds(row, 1), :]
    scale = mod_ref[0, 3 * sub + 1, pl.ds(row, 1), :]
    gate = mod_ref[0, 3 * sub + 2, pl.ds(row, 1), :]
    g = g_ref[0, sub:sub + 1, :]
    y = x * lax.rsqrt(jnp.mean(x * x, axis=-1, keepdims=True) + NORM_EPS)
    h = (y * g) * (1.0 + scale) + shift
    return h.astype(bf16), gate


def _mod_row(is_ctx):
    return CTX_ROW if is_ctx else pl.program_id(0)


def _ada_kernel(c_ref, w_ref, b_ref, o_ref):
    s = _silu(c_ref[...]).astype(bf16)
    w = w_ref[0].astype(bf16)
    out = jnp.dot(s, w, preferred_element_type=f32)
    for j in range(ADA_GROUP):
        o_ref[0, j] = out[:, j * D_MODEL:(j + 1) * D_MODEL] + b_ref[0, j]


def _ada_table(c, c_ctx, ada_w, ada_b):
    batch = c.shape[0]
    assert batch <= CTX_ROW < MOD_ROWS and N_MOD % ADA_GROUP == 0
    cvec = jnp.zeros((MOD_ROWS, D_MODEL), f32).at[:batch].set(c).at[CTX_ROW].set(c_ctx)
    bias = ada_b.reshape(DEPTH, N_MOD, 1, D_MODEL)
    return pl.pallas_call(
        _ada_kernel,
        grid=(DEPTH, N_MOD // ADA_GROUP),
        in_specs=[
            pl.BlockSpec((MOD_ROWS, D_MODEL), lambda i, n: (0, 0)),
            pl.BlockSpec((1, D_MODEL, ADA_GROUP * D_MODEL), lambda i, n: (i, 0, n)),
            pl.BlockSpec((1, ADA_GROUP, 1, D_MODEL), lambda i, n: (i, n, 0, 0)),
        ],
        out_specs=pl.BlockSpec((1, ADA_GROUP, MOD_ROWS, D_MODEL), lambda i, n: (i, n, 0, 0)),
        out_shape=jax.ShapeDtypeStruct((DEPTH, N_MOD, MOD_ROWS, D_MODEL), f32),
        compiler_params=_params(2),
        name="ada_table",
    )(cvec, ada_w, bias)


def _mod_spec(layer):
    return pl.BlockSpec((1, N_MOD, MOD_ROWS, D_MODEL), lambda *_: (layer, 0, 0, 0))


def _g_spec(layer):
    return pl.BlockSpec((1, 3, D_MODEL), lambda *_: (layer, 0, 0))


def _tok_spec(tm, width=D_MODEL):
    return pl.BlockSpec((1, tm, width), lambda b, t: (b, t, 0))


def _ffn_kernel(x_ref, mod_ref, g_ref, wi_hbm, wo_hbm, fg_ref, o_ref, wi_b, wo_b, stage_g, stage_u, stage_o, sems,
                *, layer, sub, blocks_per_batch, final):
    def chunk_copies(c, slot):
        lo = c * FFN_CHUNK
        return (
            pltpu.make_async_copy(wi_hbm.at[layer, :, pl.ds(lo, FFN_CHUNK)], stage_g.at[slot], sems.at[0, slot]),
            pltpu.make_async_copy(wi_hbm.at[layer, :, pl.ds(D_FF + lo, FFN_CHUNK)], stage_u.at[slot],
                                  sems.at[1, slot]),
            pltpu.make_async_copy(wo_hbm.at[layer, pl.ds(lo, FFN_CHUNK), :], stage_o.at[slot], sems.at[2, slot]),
        )

    def body(stream_weights):
        if stream_weights:
            for slot in range(WEIGHT_SLOTS):
                for copy in chunk_copies(slot, slot):
                    copy.start()
        x = x_ref[...]
        row = CTX_ROW if blocks_per_batch is None else pl.program_id(0) // blocks_per_batch
        h, gate = _modulated(x, mod_ref, g_ref, sub, row)
        acc = jnp.zeros(x.shape, f32)
        for c in range(N_FFN_CHUNKS):
            if stream_weights:
                slot = c % WEIGHT_SLOTS
                for copy in chunk_copies(c, slot):
                    copy.wait()
                wi_b[c, :, :FFN_CHUNK] = stage_g[slot].astype(bf16)
                wi_b[c, :, FFN_CHUNK:] = stage_u[slot].astype(bf16)
                wo_b[c] = stage_o[slot].astype(bf16)
                if c + WEIGHT_SLOTS < N_FFN_CHUNKS:
                    for copy in chunk_copies(c + WEIGHT_SLOTS, slot):
                        copy.start()
            gu = jnp.dot(h, wi_b[c], preferred_element_type=f32)
            a = _silu(gu[:, :FFN_CHUNK]) * gu[:, FFN_CHUNK:]
            acc = acc + jnp.dot(a.astype(bf16), wo_b[c], preferred_element_type=f32)
        out = x + 0.5 * gate * acc
        if final:
            out = out * lax.rsqrt(jnp.mean(out * out, axis=-1, keepdims=True) + NORM_EPS) * fg_ref[...]
        o_ref[...] = out

    first = pl.program_id(0) == 0
    pl.when(first)(functools.partial(body, True))
    pl.when(jnp.logical_not(first))(functools.partial(body, False))


def _ffn(x, mods, norm_g, wi, wo, final_g, *, layer, sub, is_ctx, final=False):
    batch, length, _ = x.shape
    tm = min(batch * length, CTX_FFN_TOKEN_BLOCK) if is_ctx else FFN_TOKEN_BLOCK
    xf = x.reshape(batch * length, D_MODEL)
    assert xf.shape[0] % tm == 0 and (is_ctx or length % tm == 0)
    spec = pl.BlockSpec((tm, D_MODEL), lambda i: (i, 0))
    hbm = pl.BlockSpec(memory_space=pl.ANY)
    out = pl.pallas_call(
        functools.partial(_ffn_kernel, layer=layer, sub=sub, blocks_per_batch=None if is_ctx else length // tm,
                          final=final),
        grid=(xf.shape[0] // tm,),
        in_specs=[spec, _mod_spec(layer), _g_spec(layer), hbm, hbm, pl.BlockSpec((1, D_MODEL), lambda i: (0, 0))],
        out_specs=spec,
        out_shape=jax.ShapeDtypeStruct(xf.shape, f32),
        scratch_shapes=[
            pltpu.VMEM((N_FFN_CHUNKS, D_MODEL, 2 * FFN_CHUNK), bf16),
            pltpu.VMEM((N_FFN_CHUNKS, FFN_CHUNK, D_MODEL), bf16),
            pltpu.VMEM((WEIGHT_SLOTS, D_MODEL, FFN_CHUNK), f32),
            pltpu.VMEM((WEIGHT_SLOTS, D_MODEL, FFN_CHUNK), f32),
            pltpu.VMEM((WEIGHT_SLOTS, FFN_CHUNK, D_MODEL), f32),
            pltpu.SemaphoreType.DMA((3, WEIGHT_SLOTS)),
        ],
        compiler_params=_params(1),
        name=f"ffn_l{layer}_s{sub}_{'ctx' if is_ctx else 'lat'}",
    )(xf, mods, norm_g, wi, wo, final_g.reshape(1, D_MODEL))
    return out.reshape(x.shape)


def _glu_kernel(x_ref, mod_ref, g_ref, w_ref, b_ref, u_ref, *, is_ctx):
    h, _ = _modulated(x_ref[0], mod_ref, g_ref, 1, _mod_row(is_ctx))
    a = jnp.dot(h, w_ref[...], preferred_element_type=f32) + b_ref[...]
    u_ref[0] = a[:, :D_MODEL] * _sigmoid(a[:, D_MODEL:])


def _conv_kernel(um_ref, up_ref, un_ref, x_ref, mod_ref, dww_ref, dwb_ref, lng_ref, lnb_ref,
                 w2_ref, b2_ref, o_ref, buf, conv, *, is_ctx):
    t = pl.program_id(1)
    tm = um_ref.shape[1]
    for j in range(N_SLABS):
        sl = slice(j * LANES, (j + 1) * LANES)
        buf[j, 0:HALO, :] = jnp.where(t > 0, up_ref[0, :, sl], 0.0)
        buf[j, HALO:HALO + tm, :] = um_ref[0, :, sl]
        buf[j, HALO + tm:, :] = jnp.where(t < pl.num_programs(1) - 1, un_ref[0, :, sl], 0.0)

    def step(i, carry):
        j = i % N_SLABS
        base = (i // N_SLABS) * CONV_ROWS
        for pair in range(CONV_ROWS // (2 * VREG_ROWS)):
            lo = base + pair * 2 * VREG_ROWS
            accs = [None] * CONV_PHASES
            taps = {}
            for k in range(CONV_WIDTH):
                wk = dww_ref[j, k * 2 * SUBLANES:(k + 1) * 2 * SUBLANES, :].astype(f32)
                for p in range(CONV_PHASES):
                    q = p + k - CONV_PAD
                    if q not in taps:
                        rows = [buf[j, pl.ds(lo + g * VREG_ROWS + HALO + q, SUBLANES, stride=CONV_PHASES), :]
                                for g in range(2)]
                        taps[q] = jnp.concatenate(rows, axis=0).astype(bf16)
                    term = taps[q].astype(f32) * wk
                    accs[p] = term if accs[p] is None else accs[p] + term
            for p in range(CONV_PHASES):
                for g in range(2):
                    conv[j, pl.ds(lo + g * VREG_ROWS + p, SUBLANES, stride=CONV_PHASES), :] = (
                        accs[p][g * SUBLANES:(g + 1) * SUBLANES])
        return carry

    lax.fori_loop(0, N_SLABS * (tm // CONV_ROWS), step, 0)

    u = jnp.concatenate([conv[j] for j in range(N_SLABS)], axis=1) + dwb_ref[...]
    mu = jnp.mean(u, axis=-1, keepdims=True)
    uc = u - mu
    var = jnp.mean(uc * uc, axis=-1, keepdims=True)
    un = (uc * lax.rsqrt(var + NORM_EPS)) * lng_ref[...] + lnb_ref[...]
    y = jnp.dot(_silu(un).astype(bf16), w2_ref[...], preferred_element_type=f32) + b2_ref[...]
    gate = mod_ref[0, 5, pl.ds(_mod_row(is_ctx), 1), :]
    o_ref[0] = x_ref[0] + gate * y


def _conv_mixer(x, mods, norm_g, pw1_w, pw1_b, dw_w, dw_b, ln_g, ln_b, pw2_w, pw2_b, *, layer, member, is_ctx):
    batch, length, _ = x.shape
    tm = min(TOKEN_BLOCK, length)
    assert length % tm == 0 and tm % HALO == 0 and tm % CONV_ROWS == 0
    tag = "ctx" if is_ctx else "lat"
    row = lambda v: v.reshape(1, -1)
    tp = min(PROJ_TOKEN_BLOCK, length)
    assert length % tp == 0
    u = pl.pallas_call(
        functools.partial(_glu_kernel, is_ctx=is_ctx),
        grid=(batch, length // tp),
        in_specs=[
            _tok_spec(tp), _mod_spec(layer), _g_spec(layer),
            _resident(pw1_w, member), pl.BlockSpec((1, 2 * D_MODEL), lambda b, t: (0, 0)),
        ],
        out_specs=_tok_spec(tp),
        out_shape=jax.ShapeDtypeStruct(x.shape, f32),
        compiler_params=_params(2),
        name=f"conv_glu_l{layer}_{tag}",
    )(x, mods, norm_g, pw1_w, row(pw1_b))

    halo_per_block = tm // HALO
    n_halo = length // HALO
    vec = pl.BlockSpec((1, D_MODEL), lambda b, t: (0, 0))
    dww = jnp.repeat(dw_w.reshape(CONV_WIDTH, N_SLABS, LANES).transpose(1, 0, 2), 2 * SUBLANES, axis=1).astype(bf16)
    return pl.pallas_call(
        functools.partial(_conv_kernel, is_ctx=is_ctx),
        grid=(batch, length // tm),
        in_specs=[
            _tok_spec(tm),
            pl.BlockSpec((1, HALO, D_MODEL), lambda b, t: (b, jnp.maximum(t * halo_per_block - 1, 0), 0)),
            pl.BlockSpec((1, HALO, D_MODEL),
                         lambda b, t: (b, jnp.minimum((t + 1) * halo_per_block, n_halo - 1), 0)),
            _tok_spec(tm), _mod_spec(layer),
            pl.BlockSpec(dww.shape, lambda b, t: (0, 0, 0)),
            vec, vec, vec, _resident(pw2_w, member), vec,
        ],
        out_specs=_tok_spec(tm),
        out_shape=jax.ShapeDtypeStruct(x.shape, f32),
        scratch_shapes=[pltpu.VMEM((N_SLABS, tm + 2 * HALO, LANES), f32), pltpu.VMEM((N_SLABS, tm, LANES), f32)],
        compiler_params=_params(2),
        name=f"conv_dw_l{layer}_{tag}",
    )(u, u, u, x, mods, dww, row(dw_b), row(ln_g), row(ln_b), pw2_w, row(pw2_b))


def _rope_tables(n_tok):
    n_freq = HEAD_DIM // 4
    pos = jnp.arange(n_tok)
    inv = ROPE_BASE ** (-jnp.arange(n_freq, dtype=f32) / n_freq)
    row = (pos // GRID_W).astype(f32)[:, None] * inv
    col = (pos % GRID_W).astype(f32)[:, None] * inv
    cos = jnp.concatenate([jnp.cos(row), jnp.cos(row), jnp.cos(col), jnp.cos(col)], axis=1)
    sin = jnp.concatenate([-jnp.sin(row), jnp.sin(row), -jnp.sin(col), jnp.sin(col)], axis=1)
    reps = LANES // HEAD_DIM
    return jnp.tile(cos, (1, reps)), jnp.tile(sin, (1, reps))


def _qkv_kernel(x_ref, mod_ref, g_ref, w_ref, cos_ref, sin_ref, q_ref, kv_ref):
    h, _ = _modulated(x_ref[0], mod_ref, g_ref, 1, _mod_row(False))
    qkv = jnp.dot(h, w_ref[...], preferred_element_type=f32)
    cos = cos_ref[...]
    sin = sin_ref[...]
    n_freq = HEAD_DIM // 4
    lane = lax.broadcasted_iota(jnp.int32, cos.shape, 1)
    is_x1 = (lane % (2 * n_freq)) < n_freq

    def rope(xb):
        partner = jnp.where(is_x1, pltpu.roll(xb, LANES - n_freq, 1), pltpu.roll(xb, n_freq, 1))
        return xb * cos + partner * sin

    for j in range(Q_DIM // LANES):
        sl = slice(j * LANES, (j + 1) * LANES)
        q_ref[0, :, sl] = (rope(qkv[:, sl]) * (HEAD_DIM ** -0.5 * LOG2E)).astype(bf16)
    for j in range(KV_DIM // LANES):
        sl = slice(j * LANES, (j + 1) * LANES)
        kv_ref[0, :, sl] = rope(qkv[:, Q_DIM + j * LANES:Q_DIM + (j + 1) * LANES]).astype(bf16)
    kv_ref[0, :, KV_DIM:] = qkv[:, Q_DIM + KV_DIM:].astype(bf16)


def _kv_kernel(x_ref, mod_ref, g_ref, w_ref, kv_ref):
    h, _ = _modulated(x_ref[0], mod_ref, g_ref, 1, _mod_row(True))
    kv_ref[0] = jnp.dot(h, w_ref[:, Q_DIM:], preferred_element_type=f32).astype(bf16)


def _attn_kernel(q_ref, kvp_ref, kvc_ref, kvn_ref, kvx_ref, sink_ref, wo_ref, x_ref, mod_ref, o_ref, o_scr):
    n = pl.program_id(1)
    n_steps = pl.num_programs(1)
    kv_ext = jnp.concatenate([kvp_ref[0], kvc_ref[0], kvn_ref[0]], axis=0)
    k_ext, v_ext = kv_ext[:, :KV_DIM], kv_ext[:, KV_DIM:]
    k_ctx, v_ctx = kvx_ref[0, :, :KV_DIM], kvx_ref[0, :, KV_DIM:]

    cols = GROUP * Q_BLOCK
    key = lax.broadcasted_iota(jnp.int32, (Q_BLOCK, cols), 0)
    qry = lax.broadcasted_iota(jnp.int32, (Q_BLOCK, cols), 1) % Q_BLOCK
    tri_prev = jnp.where(key >= qry, 0.0, MASK_VALUE)
    tri_next = jnp.where(key <= qry, 0.0, MASK_VALUE)
    edge_prev = jnp.where(n > 0, tri_prev, MASK_VALUE)
    edge_next = jnp.where(n < n_steps - 1, tri_next, MASK_VALUE)

    contract_last = (((1,), (1,)), ((), ()))
    contract_first = (((0,), (0,)), ((), ()))
    units = [(b, h) for b in range(Q_BLOCKS_PER_STEP) for h in range(N_KV_HEADS)]

    def scores(b, h):
        q = q_ref[0, b * Q_BLOCK:(b + 1) * Q_BLOCK, :]
        qs = jnp.concatenate([q[:, a * HEAD_DIM:(a + 1) * HEAD_DIM]
                              for a in range(h * GROUP, (h + 1) * GROUP)], axis=0)
        hs = slice(h * HEAD_DIM, (h + 1) * HEAD_DIM)
        k_all = jnp.concatenate([k_ext[b * Q_BLOCK:(b + 3) * Q_BLOCK, hs], k_ctx[:, hs]], axis=0)
        return lax.dot_general(k_all, qs, contract_last, preferred_element_type=f32)

    s_ahead = scores(*units[0])
    for u, (b, h) in enumerate(units):
        s = s_ahead
        if u + 1 < len(units):
            s_ahead = scores(*units[u + 1])
        heads = [h * GROUP + g for g in range(GROUP)]
        hs = slice(h * HEAD_DIM, (h + 1) * HEAD_DIM)
        v_all = jnp.concatenate([v_ext[b * Q_BLOCK:(b + 3) * Q_BLOCK, hs], v_ctx[:, hs]], axis=0)
        bias_prev = edge_prev if b == 0 else tri_prev
        bias_next = edge_next if b == Q_BLOCKS_PER_STEP - 1 else tri_next
        s_prev = s[0:Q_BLOCK] + bias_prev
        s_cur = s[Q_BLOCK:2 * Q_BLOCK]
        s_next = s[2 * Q_BLOCK:3 * Q_BLOCK] + bias_next
        s_ctx = s[3 * Q_BLOCK:]
        sink = sink_ref[h] * LOG2E
        m = jnp.maximum(
            jnp.maximum(jnp.max(s_prev, axis=0, keepdims=True), jnp.max(s_cur, axis=0, keepdims=True)),
            jnp.maximum(jnp.max(s_next, axis=0, keepdims=True), jnp.max(s_ctx, axis=0, keepdims=True)))
        m = jnp.maximum(m, sink)
        e = jnp.exp2(jnp.concatenate([s_prev, s_cur, s_next, s_ctx], axis=0) - m)
        denom = jnp.sum(e, axis=0, keepdims=True) + jnp.exp2(sink - m)
        o_t = lax.dot_general(v_all, e.astype(bf16), contract_first, preferred_element_type=f32)
        o_t = o_t * (1.0 / denom)
        for pair in range(GROUP // 2):
            g = 2 * pair
            two = jnp.concatenate([o_t[:, g * Q_BLOCK:(g + 1) * Q_BLOCK],
                                   o_t[:, (g + 1) * Q_BLOCK:(g + 2) * Q_BLOCK]], axis=0)
            o_scr[b * Q_BLOCK:(b + 1) * Q_BLOCK, heads[g] * HEAD_DIM:(heads[g] + 2) * HEAD_DIM] = two.T

    y = jnp.dot(o_scr[...].astype(bf16), wo_ref[...], preferred_element_type=f32)
    gate = mod_ref[0, 5, pl.ds(pl.program_id(0), 1), :]
    o_ref[0] = x_ref[0] + gate * y


def _attn_mixer(x, ctx, mods, norm_g, w_qkv_b, w_o_b, sink, *, layer, member):
    batch, seq, _ = x.shape
    n_ctx = ctx.shape[1]
    assert seq % PROJ_TOKEN_BLOCK == 0 and seq % Q_STEP == 0 and Q_BLOCK == WINDOW
    cos, sin = _rope_tables(seq)
    tm = PROJ_TOKEN_BLOCK
    q, kv = pl.pallas_call(
        _qkv_kernel,
        grid=(batch, seq // tm),
        in_specs=[
            _tok_spec(tm), _mod_spec(layer), _g_spec(layer), _resident(w_qkv_b, member),
            pl.BlockSpec((tm, LANES), lambda b, t: (t, 0)),
            pl.BlockSpec((tm, LANES), lambda b, t: (t, 0)),
        ],
        out_specs=[_tok_spec(tm, Q_DIM), _tok_spec(tm, 2 * KV_DIM)],
        out_shape=[jax.ShapeDtypeStruct((batch, seq, Q_DIM), bf16),
                   jax.ShapeDtypeStruct((batch, seq, 2 * KV_DIM), bf16)],
        compiler_params=_params(2),
        name=f"attn_qkv_l{layer}",
    )(x, mods, norm_g, w_qkv_b, cos, sin)

    kv_ctx = pl.pallas_call(
        _kv_kernel,
        grid=(batch, 1),
        in_specs=[_tok_spec(n_ctx), _mod_spec(layer), _g_spec(layer), _resident(w_qkv_b, member)],
        out_specs=_tok_spec(n_ctx, 2 * KV_DIM),
        out_shape=jax.ShapeDtypeStruct((batch, n_ctx, 2 * KV_DIM), bf16),
        compiler_params=_params(2),
        name=f"attn_kv_ctx_l{layer}",
    )(ctx, mods, norm_g, w_qkv_b)

    n_blocks = seq // Q_BLOCK
    halo = lambda f: pl.BlockSpec((1, Q_BLOCK, 2 * KV_DIM), f)
    prev = halo(lambda b, n: (b, jnp.maximum(n * Q_BLOCKS_PER_STEP - 1, 0), 0))
    cur = pl.BlockSpec((1, Q_STEP, 2 * KV_DIM), lambda b, n: (b, n, 0))
    nxt = halo(lambda b, n: (b, jnp.minimum((n + 1) * Q_BLOCKS_PER_STEP, n_blocks - 1), 0))
    ctx_spec = pl.BlockSpec((1, n_ctx, 2 * KV_DIM), lambda b, n: (b, 0, 0))
    sink_cols = jnp.repeat(sink.reshape(N_KV_HEADS, GROUP), Q_BLOCK, axis=1).reshape(N_KV_HEADS, 1, GROUP * Q_BLOCK)
    return pl.pallas_call(
        _attn_kernel,
        grid=(batch, seq // Q_STEP),
        in_specs=[
            _tok_spec(Q_STEP, Q_DIM), prev, cur, nxt, ctx_spec,
            pl.BlockSpec(sink_cols.shape, lambda b, n: (0, 0, 0)),
            _resident(w_o_b, member), _tok_spec(Q_STEP), _mod_spec(layer),
        ],
        out_specs=_tok_spec(Q_STEP),
        out_shape=jax.ShapeDtypeStruct(x.shape, f32),
        scratch_shapes=[pltpu.VMEM((Q_STEP, Q_DIM), f32)],
        compiler_params=_params(2),
        name=f"attn_core_l{layer}",
    )(q, kv, kv, kv, kv_ctx, sink_cols, w_o_b, x, mods)


def kernel(x, c, ctx, c_ctx, norm_g, ada_w, ada_b, ffn1_wi, ffn1_wo, ffn2_wi, ffn2_wo, conv_pw1_w, conv_pw1_b, conv_dw_w, conv_dw_b, conv_ln_g, conv_ln_b, conv_pw2_w, conv_pw2_b, attn_w_qkv, attn_w_o, attn_sink, final_g):
    mods = _ada_table(c, c_ctx, ada_w, ada_b)
    wi1, wo1, wi2, wo2 = ffn1_wi, ffn1_wo, ffn2_wi, ffn2_wo
    pw1_b16, pw2_b16, qkv_b16, o_b16 = (_to_bf16(w) for w in (conv_pw1_w, conv_pw2_w, attn_w_qkv, attn_w_o))
    for i in range(DEPTH):
        last = i == DEPTH - 1
        j = i // 2
        ffn = functools.partial(_ffn, mods=mods, norm_g=norm_g, final_g=final_g, layer=i)

        x = ffn(x, wi=wi1, wo=wo1, sub=0, is_ctx=False)
        ctx = ffn(ctx, wi=wi1, wo=wo1, sub=0, is_ctx=True)

        if i % 2 == 0:
            conv = functools.partial(
                _conv_mixer, mods=mods, norm_g=norm_g, pw1_w=pw1_b16, pw1_b=conv_pw1_b[j],
                dw_w=conv_dw_w[j], dw_b=conv_dw_b[j], ln_g=conv_ln_g[j], ln_b=conv_ln_b[j],
                pw2_w=pw2_b16, pw2_b=conv_pw2_b[j], layer=i, member=j)
            x = conv(x, is_ctx=False)
            if not last:
                ctx = conv(ctx, is_ctx=True)
        else:
            assert last, "context outputs of the attention mixer are only needed by a following layer"
            x = _attn_mixer(x, ctx, mods, norm_g, qkv_b16, o_b16, attn_sink[j], layer=i, member=j)

        x = ffn(x, wi=wi2, wo=wo2, sub=2, is_ctx=False, final=last)
        if not last:
            ctx = ffn(ctx, wi=wi2, wo=wo2, sub=2, is_ctx=True)
    return x
```

```python
import functools

import jax
import jax.numpy as jnp
from jax import lax
from jax.experimental import pallas as pl
from jax.experimental.pallas import tpu as pltpu

D_MODEL = 1024
D_FF = 2816
DEPTH = 2
GRID_W = 64
CONV_WIDTH = 31
CONV_PAD = (CONV_WIDTH - 1) // 2
N_Q_HEADS = 16
N_KV_HEADS = 4
GROUP = N_Q_HEADS // N_KV_HEADS
HEAD_DIM = 64
WINDOW = 128
ROPE_BASE = 10000.0
NORM_EPS = 1e-6
N_MOD = 9
MIXER_SUB = 1
Q_DIM = N_Q_HEADS * HEAD_DIM
KV_DIM = N_KV_HEADS * HEAD_DIM

LANES = 128
SUBLANES = 8
MOD_ROWS = 8
CTX_ROW = 4
ADA_GROUP = 3
FFN_CHUNK = 256
N_FFN_CHUNKS = D_FF // FFN_CHUNK
TOKEN_BLOCK = 512
PROJ_TOKEN_BLOCK = 1024
HALO = 16
CONV_PHASES = 4
VREG_ROWS = CONV_PHASES * SUBLANES
CONV_ROWS = 4 * VREG_ROWS
N_SLABS = D_MODEL // LANES
LOG2E = 1.4426950408889634
Q_BLOCK = 128
Q_BLOCKS_PER_STEP = 8
Q_STEP = Q_BLOCK * Q_BLOCKS_PER_STEP
FFN_TOKEN_BLOCK = 512
CTX_FFN_TOKEN_BLOCK = 1024
WEIGHT_SLOTS = 2
MASK_VALUE = -1e30
VMEM_LIMIT_BYTES = 56 * 1024 * 1024

f32 = jnp.float32
bf16 = jnp.bfloat16


def _sigmoid(x):
    return 1.0 / (1.0 + jnp.exp(-x))


def _silu(x):
    return x * _sigmoid(x)


def _params(n_grid):
    return pltpu.CompilerParams(
        dimension_semantics=("arbitrary",) * n_grid,
        vmem_limit_bytes=VMEM_LIMIT_BYTES)


def _resident(stack, index):
    return pl.BlockSpec((None,) + stack.shape[1:], lambda *_: (index, 0, 0), pipeline_mode=pl.Buffered(1))


def _modulated(x, mod_ref, g_ref, sub, row):
    shift = mod_ref[0, 3 * sub + 0, pl.ds(row, 1), :]
    scale = mod_ref[0, 3 * sub + 1, pl.ds(row, 1), :]
    gate = mod_ref[0, 3 * sub + 2, pl.ds(row, 1), :]
    g = g_ref[0, sub:sub + 1, :]
    y = x * lax.rsqrt(jnp.mean(x * x, axis=-1, keepdims=True) + NORM_EPS)
    h = (y * g) * (1.0 + scale) + shift
    return h.astype(bf16), gate


def _mod_row(is_ctx):
    return CTX_ROW if is_ctx else pl.program_id(0)


def _ada_kernel(c_ref, w_ref, b_ref, o_ref):
    s = _silu(c_ref[...]).astype(bf16)
    w = w_ref[0].astype(bf16)
    out = jnp.dot(s, w, preferred_element_type=f32)
    for j in range(ADA_GROUP):
        o_ref[0, j] = out[:, j * D_MODEL:(j + 1) * D_MODEL] + b_ref[0, j]


def _ada_table(c, c_ctx, ada_w, ada_b):
    batch = c.shape[0]
    assert batch <= CTX_ROW < MOD_ROWS and N_MOD % ADA_GROUP == 0
    cvec = jnp.zeros((MOD_ROWS, D_MODEL), f32).at[:batch].set(c).at[CTX_ROW].set(c_ctx)
    bias = ada_b.reshape(DEPTH, N_MOD, 1, D_MODEL)
    return pl.pallas_call(
        _ada_kernel,
        grid=(DEPTH, N_MOD // ADA_GROUP),
        in_specs=[
            pl.BlockSpec((MOD_ROWS, D_MODEL), lambda i, n: (0, 0)),
            pl.BlockSpec((1, D_MODEL, ADA_GROUP * D_MODEL), lambda i, n: (i, 0, n)),
            pl.BlockSpec((1, ADA_GROUP, 1, D_MODEL), lambda i, n: (i, n, 0, 0)),
        ],
        out_specs=pl.BlockSpec((1, ADA_GROUP, MOD_ROWS, D_MODEL), lambda i, n: (i, n, 0, 0)),
        out_shape=jax.ShapeDtypeStruct((DEPTH, N_MOD, MOD_ROWS, D_MODEL), f32),
        compiler_params=_params(2),
        name="ada_table",
    )(cvec, ada_w, bias)


def _mod_spec(layer):
    return pl.BlockSpec((1, N_MOD, MOD_ROWS, D_MODEL), lambda *_: (layer, 0, 0, 0))


def _g_spec(layer):
    return pl.BlockSpec((1, 3, D_MODEL), lambda *_: (layer, 0, 0))


def _tok_spec(tm, width=D_MODEL):
    return pl.BlockSpec((1, tm, width), lambda b, t: (b, t, 0))


def _ffn_kernel(x_ref, mod_ref, g_ref, wi_hbm, wo_hbm, fg_ref, o_ref, wi_b, wo_b, stage_g, stage_u, stage_o, sems,
                *, layer, sub, blocks_per_batch, final):
    def chunk_copies(c, slot):
        lo = c * FFN_CHUNK
        return (
            pltpu.make_async_copy(wi_hbm.at[layer, :, pl.ds(lo, FFN_CHUNK)], stage_g.at[slot], sems.at[0, slot]),
            pltpu.make_async_copy(wi_hbm.at[layer, :, pl.ds(D_FF + lo, FFN_CHUNK)], stage_u.at[slot],
                                  sems.at[1, slot]),
            pltpu.make_async_copy(wo_hbm.at[layer, pl.ds(lo, FFN_CHUNK), :], stage_o.at[slot], sems.at[2, slot]),
        )

    def body(stream_weights):
        if stream_weights:
            for slot in range(WEIGHT_SLOTS):
                for copy in chunk_copies(slot, slot):
                    copy.start()
        x = x_ref[...]
        row = CTX_ROW if blocks_per_batch is None else pl.program_id(0) // blocks_per_batch
        h, gate = _modulated(x, mod_ref, g_ref, sub, row)
        acc = jnp.zeros(x.shape, f32)
        for c in range(N_FFN_CHUNKS):
            if stream_weights:
                slot = c % WEIGHT_SLOTS
                for copy in chunk_copies(c, slot):
                    copy.wait()
                wi_b[c, :, :FFN_CHUNK] = stage_g[slot].astype(bf16)
                wi_b[c, :, FFN_CHUNK:] = stage_u[slot].astype(bf16)
                wo_b[c] = stage_o[slot].astype(bf16)
                if c + WEIGHT_SLOTS < N_FFN_CHUNKS:
                    for copy in chunk_copies(c + WEIGHT_SLOTS, slot):
                        copy.start()
            gu = jnp.dot(h, wi_b[c], preferred_element_type=f32)
            a = _silu(gu[:, :FFN_CHUNK]) * gu[:, FFN_CHUNK:]
            acc = acc + jnp.dot(a.astype(bf16), wo_b[c], preferred_element_type=f32)
        out = x + 0.5 * gate * acc
        if final:
            out = out * lax.rsqrt(jnp.mean(out * out, axis=-1, keepdims=True) + NORM_EPS) * fg_ref[...]
        o_ref[...] = out

    first = pl.program_id(0) == 0
    pl.when(first)(functools.partial(body, True))
    pl.when(jnp.logical_not(first))(functools.partial(body, False))


def _ffn(x, mods, norm_g, wi, wo, final_g, *, layer, sub, is_ctx, final=False):
    batch, length, _ = x.shape
    tm = min(batch * length, CTX_FFN_TOKEN_BLOCK) if is_ctx else FFN_TOKEN_BLOCK
    xf = x.reshape(batch * length, D_MODEL)
    assert xf.shape[0] % tm == 0 and (is_ctx or length % tm == 0)
    spec = pl.BlockSpec((tm, D_MODEL), lambda i: (i, 0))
    hbm = pl.BlockSpec(memory_space=pl.ANY)
    out = pl.pallas_call(
        functools.partial(_ffn_kernel, layer=layer, sub=sub, blocks_per_batch=None if is_ctx else length // tm,
                          final=final),
        grid=(xf.shape[0] // tm,),
        in_specs=[spec, _mod_spec(layer), _g_spec(layer), hbm, hbm, pl.BlockSpec((1, D_MODEL), lambda i: (0, 0))],
        out_specs=spec,
        out_shape=jax.ShapeDtypeStruct(xf.shape, f32),
        scratch_shapes=[
            pltpu.VMEM((N_FFN_CHUNKS, D_MODEL, 2 * FFN_CHUNK), bf16),
            pltpu.VMEM((N_FFN_CHUNKS, FFN_CHUNK, D_MODEL), bf16),
            pltpu.VMEM((WEIGHT_SLOTS, D_MODEL, FFN_CHUNK), f32),
            pltpu.VMEM((WEIGHT_SLOTS, D_MODEL, FFN_CHUNK), f32),
            pltpu.VMEM((WEIGHT_SLOTS, FFN_CHUNK, D_MODEL), f32),
            pltpu.SemaphoreType.DMA((3, WEIGHT_SLOTS)),
        ],
        compiler_params=_params(1),
        name=f"ffn_l{layer}_s{sub}_{'ctx' if is_ctx else 'lat'}",
    )(xf, mods, norm_g, wi, wo, final_g.reshape(1, D_MODEL))
    return out.reshape(x.shape)


def _glu_kernel(x_ref, mod_ref, g_ref, w_ref, b_ref, u_ref, *, is_ctx):
    h, _ = _modulated(x_ref[0], mod_ref, g_ref, MIXER_SUB, _mod_row(is_ctx))
    a = jnp.dot(h, w_ref[...].astype(bf16), preferred_element_type=f32) + b_ref[...]
    u_ref[0] = a[:, :D_MODEL] * _sigmoid(a[:, D_MODEL:])


def _conv_kernel(um_ref, up_ref, un_ref, x_ref, mod_ref, dww_ref, dwb_ref, lng_ref, lnb_ref,
                 w2_ref, b2_ref, o_ref, buf, conv, *, is_ctx):
    t = pl.program_id(1)
    tm = um_ref.shape[1]
    for j in range(N_SLABS):
        sl = slice(j * LANES, (j + 1) * LANES)
        buf[j, 0:HALO, :] = jnp.where(t > 0, up_ref[0, :, sl], 0.0)
        buf[j, HALO:HALO + tm, :] = um_ref[0, :, sl]
        buf[j, HALO + tm:, :] = jnp.where(t < pl.num_programs(1) - 1, un_ref[0, :, sl], 0.0)

    def step(i, carry):
        j = i % N_SLABS
        base = (i // N_SLABS) * CONV_ROWS
        for pair in range(CONV_ROWS // (2 * VREG_ROWS)):
            lo = base + pair * 2 * VREG_ROWS
            accs = [None] * CONV_PHASES
            taps = {}
            for k in range(CONV_WIDTH):
                wk = dww_ref[j, k * 2 * SUBLANES:(k + 1) * 2 * SUBLANES, :].astype(f32)
                for p in range(CONV_PHASES):
                    q = p + k - CONV_PAD
                    if q not in taps:
                        rows = [buf[j, pl.ds(lo + g * VREG_ROWS + HALO + q, SUBLANES, stride=CONV_PHASES), :]
                                for g in range(2)]
                        taps[q] = jnp.concatenate(rows, axis=0).astype(bf16)
                    term = taps[q].astype(f32) * wk
                    accs[p] = term if accs[p] is None else accs[p] + term
            for p in range(CONV_PHASES):
                for g in range(2):
                    conv[j, pl.ds(lo + g * VREG_ROWS + p, SUBLANES, stride=CONV_PHASES), :] = (
                        accs[p][g * SUBLANES:(g + 1) * SUBLANES])
        return carry

    lax.fori_loop(0, N_SLABS * (tm // CONV_ROWS), step, 0)

    u = jnp.concatenate([conv[j] for j in range(N_SLABS)], axis=1) + dwb_ref[...]
    mu = jnp.mean(u, axis=-1, keepdims=True)
    uc = u - mu
    var = jnp.mean(uc * uc, axis=-1, keepdims=True)
    un = (uc * lax.rsqrt(var + NORM_EPS)) * lng_ref[...] + lnb_ref[...]
    y = jnp.dot(_silu(un).astype(bf16), w2_ref[...].astype(bf16), preferred_element_type=f32) + b2_ref[...]
    gate = mod_ref[0, 3 * MIXER_SUB + 2, pl.ds(_mod_row(is_ctx), 1), :]
    o_ref[0] = x_ref[0] + gate * y


def _conv_mixer(x, mods, norm_g, pw1_w, pw1_b, dw_w, dw_b, ln_g, ln_b, pw2_w, pw2_b, *, layer, member, is_ctx):
    batch, length, _ = x.shape
    tm = min(TOKEN_BLOCK, length)
    assert length % tm == 0 and tm % HALO == 0 and tm % CONV_ROWS == 0
    tag = "ctx" if is_ctx else "lat"
    row = lambda v: v.reshape(1, -1)
    tp = min(PROJ_TOKEN_BLOCK, length)
    assert length % tp == 0
    u = pl.pallas_call(
        functools.partial(_glu_kernel, is_ctx=is_ctx),
        grid=(batch, length // tp),
        in_specs=[
            _tok_spec(tp), _mod_spec(layer), _g_spec(layer),
            _resident(pw1_w, member), pl.BlockSpec((1, 2 * D_MODEL), lambda b, t: (0, 0)),
        ],
        out_specs=_tok_spec(tp),
        out_shape=jax.ShapeDtypeStruct(x.shape, f32),
        compiler_params=_params(2),
        name=f"conv_glu_l{layer}_{tag}",
    )(x, mods, norm_g, pw1_w, row(pw1_b))

    halo_per_block = tm // HALO
    n_halo = length // HALO
    vec = pl.BlockSpec((1, D_MODEL), lambda b, t: (0, 0))
    dww = jnp.repeat(dw_w.reshape(CONV_WIDTH, N_SLABS, LANES).transpose(1, 0, 2), 2 * SUBLANES, axis=1).astype(bf16)
    return pl.pallas_call(
        functools.partial(_conv_kernel, is_ctx=is_ctx),
        grid=(batch, length // tm),
        in_specs=[
            _tok_spec(tm),
            pl.BlockSpec((1, HALO, D_MODEL), lambda b, t: (b, jnp.maximum(t * halo_per_block - 1, 0), 0)),
            pl.BlockSpec((1, HALO, D_MODEL),
                         lambda b, t: (b, jnp.minimum((t + 1) * halo_per_block, n_halo - 1), 0)),
            _tok_spec(tm), _mod_spec(layer),
            pl.BlockSpec(dww.shape, lambda b, t: (0, 0, 0)),
            vec, vec, vec, _resident(pw2_w, member), vec,
        ],
        out_specs=_tok_spec(tm),
        out_shape=jax.ShapeDtypeStruct(x.shape, f32),
        scratch_shapes=[pltpu.VMEM((N_SLABS, tm + 2 * HALO, LANES), f32), pltpu.VMEM((N_SLABS, tm, LANES), f32)],
        compiler_params=_params(2),
        name=f"conv_dw_l{layer}_{tag}",
    )(u, u, u, x, mods, dww, row(dw_b), row(ln_g), row(ln_b), pw2_w, row(pw2_b))


def _rope_tables(n_tok):
    n_freq = HEAD_DIM // 4
    pos = jnp.arange(n_tok)
    inv = ROPE_BASE ** (-jnp.arange(n_freq, dtype=f32) / n_freq)
    row = (pos // GRID_W).astype(f32)[:, None] * inv
    col = (pos % GRID_W).astype(f32)[:, None] * inv
    cos = jnp.concatenate([jnp.cos(row), jnp.cos(row), jnp.cos(col), jnp.cos(col)], axis=1)
    sin = jnp.concatenate([-jnp.sin(row), jnp.sin(row), -jnp.sin(col), jnp.sin(col)], axis=1)
    reps = LANES // HEAD_DIM
    return jnp.tile(cos, (1, reps)), jnp.tile(sin, (1, reps))


def _qkv_kernel(x_ref, mod_ref, g_ref, w_ref, cos_ref, sin_ref, q_ref, kv_ref):
    h, _ = _modulated(x_ref[0], mod_ref, g_ref, MIXER_SUB, _mod_row(False))
    qkv = jnp.dot(h, w_ref[...].astype(bf16), preferred_element_type=f32)
    cos = cos_ref[...]
    sin = sin_ref[...]
    n_freq = HEAD_DIM // 4
    lane = lax.broadcasted_iota(jnp.int32, cos.shape, 1)
    is_x1 = (lane % (2 * n_freq)) < n_freq

    def rope(xb):
        partner = jnp.where(is_x1, pltpu.roll(xb, LANES - n_freq, 1), pltpu.roll(xb, n_freq, 1))
        return xb * cos + partner * sin

    for j in range(Q_DIM // LANES):
        sl = slice(j * LANES, (j + 1) * LANES)
        q_ref[0, :, sl] = (rope(qkv[:, sl]) * (HEAD_DIM ** -0.5 * LOG2E)).astype(bf16)
    for j in range(KV_DIM // LANES):
        sl = slice(j * LANES, (j + 1) * LANES)
        kv_ref[0, :, sl] = rope(qkv[:, Q_DIM + j * LANES:Q_DIM + (j + 1) * LANES]).astype(bf16)
    kv_ref[0, :, KV_DIM:] = qkv[:, Q_DIM + KV_DIM:].astype(bf16)


def _kv_kernel(x_ref, mod_ref, g_ref, w_ref, kv_ref):
    h, _ = _modulated(x_ref[0], mod_ref, g_ref, MIXER_SUB, _mod_row(True))
    kv_ref[0] = jnp.dot(h, w_ref[:, Q_DIM:].astype(bf16), preferred_element_type=f32).astype(bf16)


def _attn_kernel(q_ref, kvp_ref, kvc_ref, kvn_ref, kvx_ref, sink_ref, wo_ref, x_ref, mod_ref, o_ref, o_scr):
    n = pl.program_id(1)
    n_steps = pl.num_programs(1)
    kv_ext = jnp.concatenate([kvp_ref[0], kvc_ref[0], kvn_ref[0]], axis=0)
    k_ext, v_ext = kv_ext[:, :KV_DIM], kv_ext[:, KV_DIM:]
    k_ctx, v_ctx = kvx_ref[0, :, :KV_DIM], kvx_ref[0, :, KV_DIM:]

    cols = GROUP * Q_BLOCK
    key = lax.broadcasted_iota(jnp.int32, (Q_BLOCK, cols), 0)
    qry = lax.broadcasted_iota(jnp.int32, (Q_BLOCK, cols), 1) % Q_BLOCK
    tri_prev = jnp.where(key >= qry, 0.0, MASK_VALUE)
    tri_next = jnp.where(key <= qry, 0.0, MASK_VALUE)
    edge_prev = jnp.where(n > 0, tri_prev, MASK_VALUE)
    edge_next = jnp.where(n < n_steps - 1, tri_next, MASK_VALUE)

    contract_last = (((1,), (1,)), ((), ()))
    contract_first = (((0,), (0,)), ((), ()))
    units = [(b, h) for b in range(Q_BLOCKS_PER_STEP) for h in range(N_KV_HEADS)]

    def scores(b, h):
        q = q_ref[0, b * Q_BLOCK:(b + 1) * Q_BLOCK, :]
        qs = jnp.concatenate([q[:, a * HEAD_DIM:(a + 1) * HEAD_DIM]
                              for a in range(h * GROUP, (h + 1) * GROUP)], axis=0)
        hs = slice(h * HEAD_DIM, (h + 1) * HEAD_DIM)
        k_all = jnp.concatenate([k_ext[b * Q_BLOCK:(b + 3) * Q_BLOCK, hs], k_ctx[:, hs]], axis=0)
        return lax.dot_general(k_all, qs, contract_last, preferred_element_type=f32)

    s_ahead = scores(*units[0])
    for u, (b, h) in enumerate(units):
        s = s_ahead
        if u + 1 < len(units):
            s_ahead = scores(*units[u + 1])
        heads = [h * GROUP + g for g in range(GROUP)]
        hs = slice(h * HEAD_DIM, (h + 1) * HEAD_DIM)
        v_all = jnp.concatenate([v_ext[b * Q_BLOCK:(b + 3) * Q_BLOCK, hs], v_ctx[:, hs]], axis=0)
        bias_prev = edge_prev if b == 0 else tri_prev
        bias_next = edge_next if b == Q_BLOCKS_PER_STEP - 1 else tri_next
        s_prev = s[0:Q_BLOCK] + bias_prev
        s_cur = s[Q_BLOCK:2 * Q_BLOCK]
        s_next = s[2 * Q_BLOCK:3 * Q_BLOCK] + bias_next
        s_ctx = s[3 * Q_BLOCK:]
        sink = sink_ref[h] * LOG2E
        m = jnp.maximum(
            jnp.maximum(jnp.max(s_prev, axis=0, keepdims=True), jnp.max(s_cur, axis=0, keepdims=True)),
            jnp.maximum(jnp.max(s_next, axis=0, keepdims=True), jnp.max(s_ctx, axis=0, keepdims=True)))
        m = jnp.maximum(m, sink)
        e = jnp.exp2(jnp.concatenate([s_prev, s_cur, s_next, s_ctx], axis=0) - m)
        denom = jnp.sum(e, axis=0, keepdims=True) + jnp.exp2(sink - m)
        o_t = lax.dot_general(v_all, e.astype(bf16), contract_first, preferred_element_type=f32)
        o_t = o_t * (1.0 / denom)
        for pair in range(GROUP // 2):
            g = 2 * pair
            two = jnp.concatenate([o_t[:, g * Q_BLOCK:(g + 1) * Q_BLOCK],
                                   o_t[:, (g + 1) * Q_BLOCK:(g + 2) * Q_BLOCK]], axis=0)
            o_scr[b * Q_BLOCK:(b + 1) * Q_BLOCK, heads[g] * HEAD_DIM:(heads[g] + 2) * HEAD_DIM] = two.T

    y = jnp.dot(o_scr[...].astype(bf16), wo_ref[...].astype(bf16), preferred_element_type=f32)
    gate = mod_ref[0, 3 * MIXER_SUB + 2, pl.ds(pl.program_id(0), 1), :]
    o_ref[0] = x_ref[0] + gate * y


def _attn_mixer(x, ctx, mods, norm_g, w_qkv, w_o, sink, *, layer, member):
    batch, seq, _ = x.shape
    n_ctx = ctx.shape[1]
    assert seq % PROJ_TOKEN_BLOCK == 0 and seq % Q_STEP == 0 and Q_BLOCK == WINDOW
    cos, sin = _rope_tables(seq)
    tm = PROJ_TOKEN_BLOCK
    q, kv = pl.pallas_call(
        _qkv_kernel,
        grid=(batch, seq // tm),
        in_specs=[
            _tok_spec(tm), _mod_spec(layer), _g_spec(layer), _resident(w_qkv, member),
            pl.BlockSpec((tm, LANES), lambda b, t: (t, 0)),
            pl.BlockSpec((tm, LANES), lambda b, t: (t, 0)),
        ],
        out_specs=[_tok_spec(tm, Q_DIM), _tok_spec(tm, 2 * KV_DIM)],
        out_shape=[jax.ShapeDtypeStruct((batch, seq, Q_DIM), bf16),
                   jax.ShapeDtypeStruct((batch, seq, 2 * KV_DIM), bf16)],
        compiler_params=_params(2),
        name=f"attn_qkv_l{layer}",
    )(x, mods, norm_g, w_qkv, cos, sin)

    kv_ctx = pl.pallas_call(
        _kv_kernel,
        grid=(batch, 1),
        in_specs=[_tok_spec(n_ctx), _mod_spec(layer), _g_spec(layer), _resident(w_qkv, member)],
        out_specs=_tok_spec(n_ctx, 2 * KV_DIM),
        out_shape=jax.ShapeDtypeStruct((batch, n_ctx, 2 * KV_DIM), bf16),
        compiler_params=_params(2),
        name=f"attn_kv_ctx_l{layer}",
    )(ctx, mods, norm_g, w_qkv)

    n_blocks = seq // Q_BLOCK
    halo = lambda f: pl.BlockSpec((1, Q_BLOCK, 2 * KV_DIM), f)
    prev = halo(lambda b, n: (b, jnp.maximum(n * Q_BLOCKS_PER_STEP - 1, 0), 0))
    cur = pl.BlockSpec((1, Q_STEP, 2 * KV_DIM), lambda b, n: (b, n, 0))
    nxt = halo(lambda b, n: (b, jnp.minimum((n + 1) * Q_BLOCKS_PER_STEP, n_blocks - 1), 0))
    ctx_spec = pl.BlockSpec((1, n_ctx, 2 * KV_DIM), lambda b, n: (b, 0, 0))
    sink_cols = jnp.repeat(sink.reshape(N_KV_HEADS, GROUP), Q_BLOCK, axis=1).reshape(N_KV_HEADS, 1, GROUP * Q_BLOCK)
    return pl.pallas_call(
        _attn_kernel,
        grid=(batch, seq // Q_STEP),
        in_specs=[
            _tok_spec(Q_STEP, Q_DIM), prev, cur, nxt, ctx_spec,
            pl.BlockSpec(sink_cols.shape, lambda b, n: (0, 0, 0)),
            _resident(w_o, member), _tok_spec(Q_STEP), _mod_spec(layer),
        ],
        out_specs=_tok_spec(Q_STEP),
        out_shape=jax.ShapeDtypeStruct(x.shape, f32),
        scratch_shapes=[pltpu.VMEM((Q_STEP, Q_DIM), f32)],
        compiler_params=_params(2),
        name=f"attn_core_l{layer}",
    )(q, kv, kv, kv, kv_ctx, sink_cols, w_o, x, mods)


def kernel(x, c, ctx, c_ctx, norm_g, ada_w, ada_b, ffn1_wi, ffn1_wo, ffn2_wi, ffn2_wo, conv_pw1_w, conv_pw1_b, conv_dw_w, conv_dw_b, conv_ln_g, conv_ln_b, conv_pw2_w, conv_pw2_b, attn_w_qkv, attn_w_o, attn_sink, final_g):
    mods = _ada_table(c, c_ctx, ada_w, ada_b)
    wi1, wo1, wi2, wo2 = ffn1_wi, ffn1_wo, ffn2_wi, ffn2_wo
    for i in range(DEPTH):
        last = i == DEPTH - 1
        j = i // 2
        ffn = functools.partial(_ffn, mods=mods, norm_g=norm_g, final_g=final_g, layer=i)

        x = ffn(x, wi=wi1, wo=wo1, sub=0, is_ctx=False)
        ctx = ffn(ctx, wi=wi1, wo=wo1, sub=0, is_ctx=True)

        if i % 2 == 0:
            conv = functools.partial(
                _conv_mixer, mods=mods, norm_g=norm_g, pw1_w=conv_pw1_w, pw1_b=conv_pw1_b[j],
                dw_w=conv_dw_w[j], dw_b=conv_dw_b[j], ln_g=conv_ln_g[j], ln_b=conv_ln_b[j],
                pw2_w=conv_pw2_w, pw2_b=conv_pw2_b[j], layer=i, member=j)
            x = conv(x, is_ctx=False)
            if not last:
                ctx = conv(ctx, is_ctx=True)
        else:
            assert last, "context outputs of the attention mixer are only needed by a following layer"
            x = _attn_mixer(x, ctx, mods, norm_g, attn_w_qkv, attn_w_o, attn_sink[j], layer=i, member=j)

        x = ffn(x, wi=wi2, wo=wo2, sub=2, is_ctx=False, final=last)
        if not last:
            ctx = ffn(ctx, wi=wi2, wo=wo2, sub=2, is_ctx=True)
    return x
```
